```python
import jax, jax.numpy as jnp
from jax import lax
import numpy as np

D_MODEL = 2048
BATCH = 1
SEQ = 8192
DEPTH = 1
DEC_BATCH = 8
DEC_SEQ = 32
PAST_LEN = 2048

CHUNK = 64
MIX_WIDTH = D_MODEL
CONV_CH = MIX_WIDTH // 2
CONV_WIDTH = 31
FOX_HEADS = 16
FOX_WIDTH = MIX_WIDTH - CONV_CH
FOX_HEAD_DIM = FOX_WIDTH // FOX_HEADS
N_MEM = 256
MEM_HEADS = 4
MEM_HEAD_DIM = D_MODEL // MEM_HEADS
D_FF = 5632
FFN_CONV_WIDTH = 3
Q_BLOCK = 128
IN_COLS = 2 * CONV_CH + 3 * FOX_WIDTH + FOX_HEADS
DEEPNORM_ALPHA = (2 * DEPTH) ** 0.25
DEEPNORM_BETA = (8 * DEPTH) ** -0.25
LN_EPS = 1e-5

kernel_name = "hybrid_conv_fox_stream_step"


def _layernorm(x, g, b):
    xf = x.astype(jnp.float32)
    mu = xf.mean(-1, keepdims=True)
    var = jnp.square(xf - mu).mean(-1, keepdims=True)
    return ((xf - mu) * lax.rsqrt(var + LN_EPS) * g.astype(jnp.float32) + b.astype(jnp.float32)).astype(x.dtype)


def _causal_dwconv(xp, w, b):
    y = lax.conv_general_dilated(xp, w[:, None, :], (1,), 'VALID',
                                 dimension_numbers=('NWC', 'WIO', 'NWC'),
                                 feature_group_count=xp.shape[-1])
    return y + b


def _fox_block(q, fq, qpos, k, v, fk, kpos):
    s = jnp.einsum('bqhd,bkhd->bhqk', q, k, preferred_element_type=jnp.float32) * (FOX_HEAD_DIM ** -0.5)
    s = s + jnp.transpose(fq, (0, 2, 1))[:, :, :, None] - jnp.transpose(fk, (0, 2, 1))[:, :, None, :]
    s = jnp.where(kpos[None, :] <= qpos[:, None], s, -jnp.inf)
    p = jax.nn.softmax(s, axis=-1)
    return jnp.einsum('bhqk,bkhd->bqhd', p.astype(v.dtype), v)


def _fox_attention(q, fq, k, v, fk, past):
    B, Tq, H, Dh = q.shape
    kpos = jnp.arange(k.shape[1])
    qpos = past + jnp.arange(Tq)
    if Tq % Q_BLOCK != 0:
        return _fox_block(q, fq, qpos, k, v, fk, kpos)
    nb = Tq // Q_BLOCK
    qb = q.reshape(B, nb, Q_BLOCK, H, Dh).swapaxes(0, 1)
    fb = fq.reshape(B, nb, Q_BLOCK, H).swapaxes(0, 1)
    pb = qpos.reshape(nb, Q_BLOCK)
    out = lax.map(lambda a: _fox_block(a[0], a[1], a[2], k, v, fk, kpos), (qb, fb, pb))
    return out.swapaxes(0, 1).reshape(B, Tq, H, Dh)


def _layer(x, mem_k, mem_v, conv_hist, k_hist, v_hist, lf_hist, ffn_hist,
           w_in, b_forget, w_conv, b_conv, g_conv_ln, b_conv_ln, w_out, g_ln1, b_ln1,
           w_mq, w_mo, g_ln2, b_ln2, w_up, w_ffn_conv, b_ffn_conv, w_down, g_ln3, b_ln3):
    B, T, _ = x.shape
    past = k_hist.shape[1]
    u = x @ w_in
    c1 = CONV_CH
    c2 = 2 * CONV_CH
    c3 = c2 + FOX_WIDTH
    c4 = c3 + FOX_WIDTH
    c5 = c4 + FOX_WIDTH
    a, gate, q, k, v, zf = jnp.split(u, [c1, c2, c3, c4, c5], axis=-1)
    glu = a * jax.nn.sigmoid(gate)
    conv_in = jnp.concatenate([conv_hist, glu], axis=1)
    ca = _causal_dwconv(conv_in, w_conv, b_conv)
    ca = jax.nn.silu(_layernorm(ca, g_conv_ln, b_conv_ln))
    new_conv = conv_in[:, -(CONV_WIDTH - 1):]
    q = q.reshape(B, T, FOX_HEADS, FOX_HEAD_DIM)
    k = k.reshape(B, T, FOX_HEADS, FOX_HEAD_DIM)
    v = v.reshape(B, T, FOX_HEADS, FOX_HEAD_DIM)
    logf = jax.nn.log_sigmoid(zf.astype(jnp.float32) + b_forget.astype(jnp.float32))
    k_all = jnp.concatenate([k_hist, k], axis=1)
    v_all = jnp.concatenate([v_hist, v], axis=1)
    F = jnp.cumsum(jnp.concatenate([lf_hist.astype(jnp.float32), logf], axis=1), axis=1)
    ob = _fox_attention(q, F[:, past:], k_all, v_all, F, past).reshape(B, T, FOX_WIDTH)
    mix = jnp.concatenate([ca.astype(x.dtype), ob.astype(x.dtype)], axis=-1) @ w_out
    x = _layernorm(DEEPNORM_ALPHA * x + mix, g_ln1, b_ln1)
    qm = (x @ w_mq).reshape(B, T, MEM_HEADS, MEM_HEAD_DIM)
    sm = jnp.einsum('bqhd,bmhd->bhqm', qm, mem_k, preferred_element_type=jnp.float32) * (MEM_HEAD_DIM ** -0.5)
    pm = jax.nn.softmax(sm, axis=-1).astype(mem_v.dtype)
    om = jnp.einsum('bhqm,bmhd->bqhd', pm, mem_v).reshape(B, T, D_MODEL)
    x = _layernorm(DEEPNORM_ALPHA * x + om @ w_mo, g_ln2, b_ln2)
    h = x @ w_up
    ffn_in = jnp.concatenate([ffn_hist, h], axis=1)
    hc = _causal_dwconv(ffn_in, w_ffn_conv, b_ffn_conv)
    hg, hv = jnp.split(hc, [D_FF], axis=-1)
    y = (jax.nn.silu(hg) * hv) @ w_down
    x = _layernorm(DEEPNORM_ALPHA * x + y, g_ln3, b_ln3)
    new_ffn = ffn_in[:, -(FFN_CONV_WIDTH - 1):]
    return x, new_conv, k, v, logf.astype(x.dtype), new_ffn


def setup_inputs(seed: int = 0) -> dict:
    key = jax.random.key(seed)
    ks = jax.random.split(key, 40)
    n = lambda i, shape, s: jax.random.normal(ks[i], shape, jnp.float32) * s
    L = DEPTH
    return {
        "x_prompt": n(0, (BATCH, SEQ, D_MODEL), 1.0),
        "x_sample": n(1, (DEC_BATCH, DEC_SEQ, D_MODEL), 1.0),
        "cache_conv": n(2, (L, DEC_BATCH, CONV_WIDTH - 1, CONV_CH), 0.5),
        "cache_fox_k": n(3, (L, DEC_BATCH, PAST_LEN, FOX_HEADS, FOX_HEAD_DIM), 1.0),
        "cache_fox_v": n(4, (L, DEC_BATCH, PAST_LEN, FOX_HEADS, FOX_HEAD_DIM), 1.0),
        "cache_fox_logf": jax.nn.log_sigmoid(n(5, (L, DEC_BATCH, PAST_LEN, FOX_HEADS), 1.0) + 3.0),
        "cache_mem_k": n(6, (L, DEC_BATCH, N_MEM, MEM_HEADS, MEM_HEAD_DIM), 1.0),
        "cache_mem_v": n(7, (L, DEC_BATCH, N_MEM, MEM_HEADS, MEM_HEAD_DIM), 1.0),
        "cache_ffn": n(8, (L, DEC_BATCH, FFN_CONV_WIDTH - 1, 2 * D_FF), 1.0),
        "mem_prompt": n(9, (BATCH, N_MEM, D_MODEL), 1.0),
        "w_in": n(10, (L, D_MODEL, IN_COLS), D_MODEL ** -0.5),
        "b_forget": jnp.linspace(1.0, 5.0, FOX_HEADS)[None, :] + n(11, (L, FOX_HEADS), 0.1),
        "w_conv": n(12, (L, CONV_WIDTH, CONV_CH), CONV_WIDTH ** -0.5),
        "b_conv": n(13, (L, CONV_CH), 0.02),
        "g_conv_ln": 1.0 + n(14, (L, CONV_CH), 0.02),
        "b_conv_ln": n(15, (L, CONV_CH), 0.02),
        "w_out": n(16, (L, MIX_WIDTH, D_MODEL), DEEPNORM_BETA * MIX_WIDTH ** -0.5),
        "g_ln1": 1.0 + n(17, (L, D_MODEL), 0.02),
        "b_ln1": n(18, (L, D_MODEL), 0.02),
        "w_mq": n(19, (L, D_MODEL, D_MODEL), D_MODEL ** -0.5),
        "w_mk": n(20, (L, D_MODEL, D_MODEL), D_MODEL ** -0.5),
        "w_mv": n(21, (L, D_MODEL, D_MODEL), D_MODEL ** -0.5),
        "w_mo": n(22, (L, D_MODEL, D_MODEL), DEEPNORM_BETA * D_MODEL ** -0.5),
        "g_ln2": 1.0 + n(23, (L, D_MODEL), 0.02),
        "b_ln2": n(24, (L, D_MODEL), 0.02),
        "w_up": n(25, (L, D_MODEL, 2 * D_FF), D_MODEL ** -0.5),
        "w_ffn_conv": n(26, (L, FFN_CONV_WIDTH, 2 * D_FF), FFN_CONV_WIDTH ** -0.5),
        "b_ffn_conv": n(27, (L, 2 * D_FF), 0.02),
        "w_down": n(28, (L, D_FF, D_MODEL), DEEPNORM_BETA * D_FF ** -0.5),
        "g_ln3": 1.0 + n(29, (L, D_MODEL), 0.02),
        "b_ln3": n(30, (L, D_MODEL), 0.02),
    }


def reference(x_prompt, x_sample, cache_conv, cache_fox_k, cache_fox_v, cache_fox_logf,
              cache_mem_k, cache_mem_v, cache_ffn, mem_prompt,
              w_in, b_forget, w_conv, b_conv, g_conv_ln, b_conv_ln, w_out, g_ln1, b_ln1,
              w_mq, w_mk, w_mv, w_mo, g_ln2, b_ln2, w_up, w_ffn_conv, b_ffn_conv, w_down,
              g_ln3, b_ln3):
    B = x_prompt.shape[0]
    yp, ys = x_prompt, x_sample
    pc, pk, pv, plf, pmk, pmv, pf = [], [], [], [], [], [], []
    sc, sk, sv, slf, sf = [], [], [], [], []
    for l in range(DEPTH):
        params = (w_in[l], b_forget[l], w_conv[l], b_conv[l], g_conv_ln[l], b_conv_ln[l], w_out[l],
                  g_ln1[l], b_ln1[l], w_mq[l], w_mo[l], g_ln2[l], b_ln2[l], w_up[l], w_ffn_conv[l],
                  b_ffn_conv[l], w_down[l], g_ln3[l], b_ln3[l])
        mk = (mem_prompt @ w_mk[l]).reshape(B, N_MEM, MEM_HEADS, MEM_HEAD_DIM)
        mv = (mem_prompt @ w_mv[l]).reshape(B, N_MEM, MEM_HEADS, MEM_HEAD_DIM)
        zc = jnp.zeros((B, CONV_WIDTH - 1, CONV_CH), yp.dtype)
        zk = jnp.zeros((B, 0, FOX_HEADS, FOX_HEAD_DIM), yp.dtype)
        zl = jnp.zeros((B, 0, FOX_HEADS), yp.dtype)
        zf = jnp.zeros((B, FFN_CONV_WIDTH - 1, 2 * D_FF), yp.dtype)
        yp, c_, k_, v_, lf_, f_ = _layer(yp, mk, mv, zc, zk, zk, zl, zf, *params)
        pc.append(c_); pk.append(k_); pv.append(v_); plf.append(lf_); pmk.append(mk); pmv.append(mv); pf.append(f_)
        ys, c_, k_, v_, lf_, f_ = _layer(ys, cache_mem_k[l], cache_mem_v[l], cache_conv[l], cache_fox_k[l],
                                         cache_fox_v[l], cache_fox_logf[l], cache_ffn[l], *params)
        sc.append(c_); sk.append(k_); sv.append(v_); slf.append(lf_); sf.append(f_)
    p_conv = jnp.stack(pc)
    p_fox_k = jnp.stack(pk)
    p_fox_v = jnp.stack(pv)
    p_fox_logf = jnp.stack(plf)
    p_mem_k = jnp.stack(pmk)
    p_mem_v = jnp.stack(pmv)
    p_ffn = jnp.stack(pf)
    s_conv = jnp.stack(sc)
    s_fox_k = jnp.stack(sk)
    s_fox_v = jnp.stack(sv)
    s_fox_logf = jnp.stack(slf)
    s_ffn = jnp.stack(sf)
    return (yp, ys, p_conv, p_fox_k, p_fox_v, p_fox_logf, p_mem_k, p_mem_v, p_ffn,
            s_conv, s_fox_k, s_fox_v, s_fox_logf, s_ffn)
```

```python
import functools

import jax
import jax.numpy as jnp
from jax import lax
from jax.experimental import pallas as pl
from jax.experimental.pallas import tpu as pltpu

BF = jnp.bfloat16
F32 = jnp.float32

LANES = 128
HIST_ROWS = 32
FFN_PAD = 8
LN_EPS = 1e-5
NEG_BIG = -1e30
MIB = 1024 * 1024


def _cparams(n_axes, vmem_mib):
    return pltpu.CompilerParams(dimension_semantics=("arbitrary",) * n_axes,
                                vmem_limit_bytes=vmem_mib * MIB)


def _dot(a, b):
    return jnp.dot(a, b, preferred_element_type=F32)


def _dot_t(a, b):
    return lax.dot_general(a, b, (((1,), (1,)), ((), ())), preferred_element_type=F32)


def _layernorm(xf, g, b):
    mu = jnp.mean(xf, axis=-1, keepdims=True)
    xc = xf - mu
    var = jnp.mean(xc * xc, axis=-1, keepdims=True)
    return xc * lax.rsqrt(var + LN_EPS) * g + b


def _log_sigmoid(z):
    return jnp.minimum(z, 0.0) - jnp.log1p(jnp.exp(-jnp.abs(z)))


def _mm_kernel(x_ref, w_ref, o_ref):
    o_ref[...] = _dot(x_ref[...].astype(BF), w_ref[...])


def _matmul(x, w, tm, tn):
    m, k = x.shape
    n = w.shape[1]
    return pl.pallas_call(
        _mm_kernel,
        grid=(m // tm, n // tn),
        in_specs=[pl.BlockSpec((tm, k), lambda i, j: (i, 0)),
                  pl.BlockSpec((k, tn), lambda i, j: (0, j))],
        out_specs=pl.BlockSpec((tm, tn), lambda i, j: (i, j)),
        out_shape=jax.ShapeDtypeStruct((m, n), F32),
        compiler_params=_cparams(2, 40),
        name="mem_proj",
    )(x, w)


def _inproj_kernel(x_ref, wa_ref, wg_ref, wq_ref, wk_ref, wv_ref, wf_ref, bf_ref,
                   glu_ref, qb_ref, kf_ref, kb_ref, vf_ref, vb_ref, lf_ref, xb_ref, *, q_scale):
    @pl.when(pl.program_id(1) == 0)
    def _():
        xb0 = x_ref[...].astype(BF)
        xb_ref[...] = xb0
        lf_ref[...] = _log_sigmoid(_dot(xb0, wf_ref[...]) + bf_ref[...])

    xb = xb_ref[...]
    glu_ref[...] = _dot(xb, wa_ref[...]) * jax.nn.sigmoid(_dot(xb, wg_ref[...]))
    qb_ref[...] = (_dot(xb, wq_ref[...]) * q_scale).astype(BF)
    k = _dot(xb, wk_ref[...])
    kf_ref[...] = k
    kb_ref[...] = k.astype(BF)
    v = _dot(xb, wv_ref[...])
    vf_ref[...] = v
    vb_ref[...] = v.astype(BF)


def _inproj(x, wa, wg, wq, wk, wv, wf, bfg, q_scale, tm, tn):
    r, d = x.shape
    c = wa.shape[1]
    wspec = pl.BlockSpec((d, tn), lambda i, j: (0, j))
    ospec = pl.BlockSpec((tm, tn), lambda i, j: (i, j))
    f32o = jax.ShapeDtypeStruct((r, c), F32)
    bf16o = jax.ShapeDtypeStruct((r, c), BF)
    return pl.pallas_call(
        functools.partial(_inproj_kernel, q_scale=q_scale),
        grid=(r // tm, c // tn),
        in_specs=[pl.BlockSpec((tm, d), lambda i, j: (i, 0)),
                  wspec, wspec, wspec, wspec, wspec,
                  pl.BlockSpec((d, LANES), lambda i, j: (0, 0)),
                  pl.BlockSpec((1, LANES), lambda i, j: (0, 0))],
        out_specs=[ospec, ospec, ospec, ospec, ospec, ospec,
                   pl.BlockSpec((tm, LANES), lambda i, j: (i, 0))],
        out_shape=[f32o, bf16o, f32o, bf16o, f32o, bf16o,
                   jax.ShapeDtypeStruct((r, LANES), F32)],
        scratch_shapes=[pltpu.VMEM((tm, d), BF)],
        compiler_params=_cparams(2, 56),
        name="in_proj",
    )(x, wa, wg, wq, wk, wv, wf, bfg)


def _conv_kernel(hist_ref, cur_ref, w_ref, b_ref, g_ref, bl_ref, o_ref, xin_ref, y_ref,
                 *, zero_first, tt, rc, width):
    hist = hist_ref[...]
    if zero_first:
        hist = jnp.where(pl.program_id(0) == 0, 0.0, hist)
    nslab = cur_ref.shape[1] // LANES
    for c in range(nslab):
        cs = slice(c * LANES, (c + 1) * LANES)
        xin_ref[c, 0:HIST_ROWS, :] = hist[:, cs]
        xin_ref[c, HIST_ROWS:HIST_ROWS + tt, :] = cur_ref[:, cs]
    first = HIST_ROWS - (width - 1)

    def slab(c, carry):
        wc = w_ref[c]
        bc = b_ref[c]
        for r0 in range(0, tt, rc):
            acc = wc[0:1, :] * xin_ref[c, r0 + first:r0 + first + rc, :]
            for j in range(1, width):
                acc = acc + wc[j:j + 1, :] * xin_ref[c, r0 + first + j:r0 + first + j + rc, :]
            y_ref[c, r0:r0 + rc, :] = acc + bc
        return carry

    lax.fori_loop(0, nslab, slab, 0)
    y = jnp.concatenate([y_ref[c] for c in range(nslab)], axis=1)
    yn = _layernorm(y, g_ref[...], bl_ref[...])
    o_ref[...] = (yn * jax.nn.sigmoid(yn)).astype(BF)


def _conv_module(hist_arr, hist_spec, glu, w, b, g, bl, zero_first, tt, width):
    r, c = glu.shape
    rc = min(tt, 64)
    nslab = c // LANES
    vec = pl.BlockSpec((1, c), lambda i: (0, 0))
    return pl.pallas_call(
        functools.partial(_conv_kernel, zero_first=zero_first, tt=tt, rc=rc, width=width),
        grid=(r // tt,),
        in_specs=[hist_spec,
                  pl.BlockSpec((tt, c), lambda i: (i, 0)),
                  pl.BlockSpec(w.shape, lambda i: (0, 0, 0)),
                  pl.BlockSpec(b.shape, lambda i: (0, 0, 0)),
                  vec, vec],
        out_specs=pl.BlockSpec((tt, c), lambda i: (i, 0)),
        out_shape=jax.ShapeDtypeStruct((r, c), BF),
        scratch_shapes=[pltpu.VMEM((nslab, HIST_ROWS + tt, LANES), F32),
                        pltpu.VMEM((nslab, tt, LANES), F32)],
        compiler_params=_cparams(1, 32),
        name="conv_module",
    )(hist_arr, glu, w, b, g, bl)


def _cumsum_kernel(x_ref, o_ref):
    x = x_ref[...]
    n = x.shape[1]
    idx = lax.broadcasted_iota(jnp.int32, x.shape, 1)
    s = 1
    while s < n:
        x = x + jnp.where(idx >= s, pltpu.roll(x, s, axis=1), 0.0)
        s *= 2
    o_ref[...] = x


def _cumsum_lanes(x):
    return pl.pallas_call(
        _cumsum_kernel,
        out_shape=jax.ShapeDtypeStruct(x.shape, F32),
        compiler_params=pltpu.CompilerParams(vmem_limit_bytes=32 * MIB),
        name="cumsum_logf",
    )(x)


def _stack_heads(q, half):
    lane = lax.broadcasted_iota(jnp.int32, q.shape, 1)
    zero = jnp.zeros_like(q)
    return jnp.concatenate([jnp.where(lane < half, q, zero), jnp.where(lane >= half, q, zero)], axis=0)


def _unstack_heads(o, t, half):
    lane = lax.broadcasted_iota(jnp.int32, (t, o.shape[1]), 1)
    return jnp.where(lane < half, o[:t], o[t:])


def _flash_kernel(q_ref, k_ref, v_ref, f_ref, o_ref, qs_ref, m_ref, l_ref, acc_ref, *, tq, half):
    qi = pl.program_id(1)
    qs_ref[...] = _stack_heads(q_ref[...], half)
    m_ref[...] = jnp.full(m_ref.shape, NEG_BIG, F32)
    l_ref[...] = jnp.zeros(l_ref.shape, F32)
    acc_ref[...] = jnp.zeros(acc_ref.shape, F32)

    def step(j, masked):
        start = pl.multiple_of(j * tq, tq)
        k = k_ref[pl.ds(start, tq), :]
        v = v_ref[pl.ds(start, tq), :]
        s = _dot_t(qs_ref[...], k)
        s0 = s[:tq] - f_ref[0:1, pl.ds(start, tq)]
        s1 = s[tq:] - f_ref[1:2, pl.ds(start, tq)]
        if masked:
            row = lax.broadcasted_iota(jnp.int32, (tq, tq), 0)
            col = lax.broadcasted_iota(jnp.int32, (tq, tq), 1)
            vis = col <= row
            s0 = jnp.where(vis, s0, NEG_BIG)
            s1 = jnp.where(vis, s1, NEG_BIG)
        s = jnp.concatenate([s0, s1], axis=0)
        m_prev = m_ref[...]
        m_new = jnp.maximum(m_prev, jnp.max(s, axis=1, keepdims=True))
        alpha = jnp.exp(m_prev - m_new)
        p = jnp.exp(s - m_new)
        l_ref[...] = alpha * l_ref[...] + jnp.sum(p, axis=1, keepdims=True)
        acc_ref[...] = alpha * acc_ref[...] + _dot(p.astype(BF), v)
        m_ref[...] = m_new

    def body(j, carry):
        step(j, False)
        return carry

    lax.fori_loop(0, qi, body, 0)
    step(qi, True)
    o_ref[...] = _unstack_heads(acc_ref[...] / l_ref[...], tq, half).astype(BF)


def _flash_attention(qb, kb, vb, ft, tq, head_dim):
    t, c = qb.shape
    npair = c // LANES
    return pl.pallas_call(
        functools.partial(_flash_kernel, tq=tq, half=head_dim),
        grid=(npair, t // tq),
        in_specs=[pl.BlockSpec((tq, LANES), lambda hp, qi: (qi, hp)),
                  pl.BlockSpec((t, LANES), lambda hp, qi: (0, hp)),
                  pl.BlockSpec((t, LANES), lambda hp, qi: (0, hp)),
                  pl.BlockSpec((None, 2, t), lambda hp, qi: (hp, 0, 0))],
        out_specs=pl.BlockSpec((tq, LANES), lambda hp, qi: (qi, hp)),
        out_shape=jax.ShapeDtypeStruct((t, c), BF),
        scratch_shapes=[pltpu.VMEM((2 * tq, LANES), BF),
                        pltpu.VMEM((2 * tq, 1), F32),
                        pltpu.VMEM((2 * tq, 1), F32),
                        pltpu.VMEM((2 * tq, LANES), F32)],
        compiler_params=_cparams(2, 40),
        name="fox_flash",
    )(qb, kb, vb, ft)


def _decode_kernel(q_ref, kn_ref, vn_ref, kc_ref, vc_ref, f_ref, o_ref, *, past, tn, half):
    qs = _stack_heads(q_ref[...], half)
    kc = kc_ref[...].astype(BF)
    vc = vc_ref[...].astype(BF)
    sc = _dot_t(qs, kc)
    sn = _dot_t(qs, kn_ref[...])
    sc = jnp.concatenate([sc[:tn] - f_ref[0:1, 0:past], sc[tn:] - f_ref[1:2, 0:past]], axis=0)
    row = lax.broadcasted_iota(jnp.int32, (tn, tn), 0)
    col = lax.broadcasted_iota(jnp.int32, (tn, tn), 1)
    vis = col <= row
    sn0 = jnp.where(vis, sn[:tn] - f_ref[0:1, past:past + tn], NEG_BIG)
    sn1 = jnp.where(vis, sn[tn:] - f_ref[1:2, past:past + tn], NEG_BIG)
    sn = jnp.concatenate([sn0, sn1], axis=0)
    m = jnp.maximum(jnp.max(sc, axis=1, keepdims=True), jnp.max(sn, axis=1, keepdims=True))
    pc = jnp.exp(sc - m)
    pn = jnp.exp(sn - m)
    l = jnp.sum(pc, axis=1, keepdims=True) + jnp.sum(pn, axis=1, keepdims=True)
    o = (_dot(pc.astype(BF), vc) + _dot(pn.astype(BF), vn_ref[...])) / l
    o_ref[...] = _unstack_heads(o, tn, half).astype(BF)


def _decode_attention(qb, kb, vb, kcache, vcache, ft, tn, head_dim):
    r, c = qb.shape
    nb, past, _ = kcache.shape
    npair = c // LANES
    lp = ft.shape[-1]
    new = pl.BlockSpec((tn, LANES), lambda b, hp: (b, hp))
    cache = pl.BlockSpec((None, past, LANES), lambda b, hp: (b, 0, hp))
    return pl.pallas_call(
        functools.partial(_decode_kernel, past=past, tn=tn, half=head_dim),
        grid=(nb, npair),
        in_specs=[new, new, new, cache, cache,
                  pl.BlockSpec((None, None, 2, lp), lambda b, hp: (b, hp, 0, 0))],
        out_specs=new,
        out_shape=jax.ShapeDtypeStruct((r, c), BF),
        compiler_params=_cparams(2, 32),
        name="fox_decode",
    )(qb, kb, vb, kcache, vcache, ft)


def _mm_ln_kernel(*refs, n_lhs, alpha):
    lhs = refs[:n_lhs]
    ws = refs[n_lhs:2 * n_lhs]
    r_ref, g_ref, b_ref, of_ref, ob_ref = refs[2 * n_lhs:]
    acc = _dot(lhs[0][...], ws[0][...])
    for a_ref, w_ref in zip(lhs[1:], ws[1:]):
        acc = acc + _dot(a_ref[...], w_ref[...])
    y = _layernorm(alpha * r_ref[...] + acc, g_ref[...], b_ref[...])
    of_ref[...] = y
    ob_ref[...] = y.astype(BF)


def _mm_ln(lhs_list, w_list, resid, g, b, alpha, tm):
    r, d = resid.shape
    n = len(lhs_list)
    row = lambda i: (i, 0)
    const = lambda i: (0, 0)
    return pl.pallas_call(
        functools.partial(_mm_ln_kernel, n_lhs=n, alpha=alpha),
        grid=(r // tm,),
        in_specs=([pl.BlockSpec((tm, a.shape[1]), row) for a in lhs_list]
                  + [pl.BlockSpec(w.shape, const) for w in w_list]
                  + [pl.BlockSpec((tm, d), row), pl.BlockSpec((1, d), const), pl.BlockSpec((1, d), const)]),
        out_specs=[pl.BlockSpec((tm, d), row), pl.BlockSpec((tm, d), row)],
        out_shape=[jax.ShapeDtypeStruct((r, d), F32), jax.ShapeDtypeStruct((r, d), BF)],
        compiler_params=_cparams(1, 48),
        name="proj_ln",
    )(*lhs_list, *w_list, resid, g, b)


def _xattn_kernel(x_ref, wq_ref, mk_ref, mv_ref, o_ref, *, heads, scale):
    x = x_ref[...]
    dh = x.shape[1] // heads
    for h in range(heads):
        hs = slice(h * dh, (h + 1) * dh)
        qm = _dot(x, wq_ref[:, hs]).astype(BF)
        s = _dot_t(qm, mk_ref[:, hs].astype(BF)) * scale
        p = jnp.exp(s - jnp.max(s, axis=1, keepdims=True))
        l = jnp.sum(p, axis=1, keepdims=True)
        o_ref[:, hs] = (_dot(p.astype(BF), mv_ref[:, hs].astype(BF)) / l).astype(BF)


def _xattn(xb, wq, mk, mv, heads, tm):
    r, d = xb.shape
    nmb, n_mem, _ = mk.shape
    mem_idx = (lambda i: (i, 0, 0)) if nmb > 1 else (lambda i: (0, 0, 0))
    return pl.pallas_call(
        functools.partial(_xattn_kernel, heads=heads, scale=(d // heads) ** -0.5),
        grid=(r // tm,),
        in_specs=[pl.BlockSpec((tm, d), lambda i: (i, 0)),
                  pl.BlockSpec((d, d), lambda i: (0, 0)),
                  pl.BlockSpec((None, n_mem, d), mem_idx),
                  pl.BlockSpec((None, n_mem, d), mem_idx)],
        out_specs=pl.BlockSpec((tm, d), lambda i: (i, 0)),
        out_shape=jax.ShapeDtypeStruct((r, d), BF),
        compiler_params=_cparams(1, 48),
        name="mem_xattn",
    )(xb, wq, mk, mv)


def _ffn_kernel(xb_ref, xf_ref, hg_ref, hv_ref, wug_ref, wuv_ref, wcg_ref, wcv_ref, bcg_ref, bcv_ref,
                wd_ref, g_ref, b_ref, o_ref, tg_ref, tv_ref, acc_ref, hbuf_ref, carry_ref,
                *, nseg, seg, use_carry, alpha):
    i = pl.program_id(0)
    f = pl.program_id(1)

    @pl.when(f == 0)
    def _():
        acc_ref[...] = jnp.zeros(acc_ref.shape, F32)

    xb = xb_ref[...]

    def conv_half(wu_ref, wc_ref, bc_ref, hist_ref, tail_ref, which):
        h = _dot(xb, wu_ref[...])
        w = wc_ref[...]
        outs = []
        for b in range(nseg):
            base = b * (seg + FFN_PAD)
            hb = h[b * seg:(b + 1) * seg]
            if use_carry:
                @pl.when(i == 0)
                def _(b=b):
                    carry_ref[which, f] = hist_ref[b]

                hist = carry_ref[which, f]
            else:
                hist = hist_ref[b]
            hbuf_ref[base:base + FFN_PAD, :] = hist
            hbuf_ref[base + FFN_PAD:base + FFN_PAD + seg, :] = hb
            p1 = hbuf_ref[base + FFN_PAD - 1:base + FFN_PAD - 1 + seg, :]
            p2 = hbuf_ref[base + FFN_PAD - 2:base + FFN_PAD - 2 + seg, :]
            outs.append(w[0:1, :] * p2 + w[1:2, :] * p1 + w[2:3, :] * hb + bc_ref[...])
            tail = hb[seg - FFN_PAD:seg]
            tail_ref[b] = tail
            if use_carry:
                carry_ref[which, f] = tail
        return outs[0] if nseg == 1 else jnp.concatenate(outs, axis=0)

    cg = conv_half(wug_ref, wcg_ref, bcg_ref, hg_ref, tg_ref, 0)
    cv = conv_half(wuv_ref, wcv_ref, bcv_ref, hv_ref, tv_ref, 1)
    act = (cg * jax.nn.sigmoid(cg) * cv).astype(BF)
    acc_ref[...] += _dot(act, wd_ref[...])

    @pl.when(f == pl.num_programs(1) - 1)
    def _():
        o_ref[...] = _layernorm(alpha * xf_ref[...] + acc_ref[...], g_ref[...], b_ref[...])


def _ffn(xb, xf, hist, wu, wc, bc, wd, g, b, alpha, nseg, seg, use_carry, tf):
    r, d = xf.shape
    tm = nseg * seg
    dff = wd.shape[0]
    nf = dff // tf
    row = lambda i, f: (i, 0)
    const = lambda i, f: (0, 0)
    lo = lambda i, f: (0, f)
    hi = lambda i, f: (0, nf + f)
    lo3 = lambda i, f: (0, 0, f)
    hi3 = lambda i, f: (0, 0, nf + f)
    ntile = r // tm
    tail = jax.ShapeDtypeStruct((ntile * nseg, FFN_PAD, dff), F32)
    tail_spec = pl.BlockSpec((nseg, FFN_PAD, tf), lambda i, f: (i, 0, f))
    y, tg, tv = pl.pallas_call(
        functools.partial(_ffn_kernel, nseg=nseg, seg=seg, use_carry=use_carry, alpha=alpha),
        grid=(r // tm, nf),
        in_specs=[pl.BlockSpec((tm, d), row), pl.BlockSpec((tm, d), row),
                  pl.BlockSpec((nseg, FFN_PAD, tf), lo3), pl.BlockSpec((nseg, FFN_PAD, tf), hi3),
                  pl.BlockSpec((d, tf), lo), pl.BlockSpec((d, tf), hi),
                  pl.BlockSpec((wc.shape[0], tf), lo), pl.BlockSpec((wc.shape[0], tf), hi),
                  pl.BlockSpec((1, tf), lo), pl.BlockSpec((1, tf), hi),
                  pl.BlockSpec((tf, d), lambda i, f: (f, 0)),
                  pl.BlockSpec((1, d), const), pl.BlockSpec((1, d), const)],
        out_specs=[pl.BlockSpec((tm, d), row), tail_spec, tail_spec],
        out_shape=[jax.ShapeDtypeStruct((r, d), F32), tail, tail],
        scratch_shapes=[pltpu.VMEM((tm, d), F32),
                        pltpu.VMEM((nseg * (seg + FFN_PAD), tf), F32),
                        pltpu.VMEM((2, nf, FFN_PAD, tf), F32)],
        compiler_params=_cparams(2, 56),
        name="conv_ffn",
    )(xb, xf, hist, hist, wu, wu, wc, wc, bc, bc, wd, g, b)
    last = (ntile - 1) * nseg
    return y, tg[last:], tv[last:]


def _pad_lanes(a, n):
    return jnp.pad(a, ((0, 0), (0, n - a.shape[1])))


def _layer_weights(w_in, b_forget, w_conv, b_conv, g_conv_ln, b_conv_ln, w_out, g_ln1, b_ln1,
                   w_mq, w_mo, g_ln2, b_ln2, w_up, w_ffn_conv, b_ffn_conv, w_down, g_ln3, b_ln3,
                   conv_ch, fox_w):
    c1, c2 = conv_ch, 2 * conv_ch
    c3, c4, c5 = c2 + fox_w, c2 + 2 * fox_w, c2 + 3 * fox_w
    wb = w_in.astype(BF)
    vec = lambda a: a[None, :]
    return dict(
        wa=wb[:, :c1], wg=wb[:, c1:c2], wq=wb[:, c2:c3], wk=wb[:, c3:c4], wv=wb[:, c4:c5],
        wf=_pad_lanes(wb[:, c5:], LANES), bf=_pad_lanes(vec(b_forget), LANES),
        w_conv=jnp.pad(w_conv, ((0, HIST_ROWS - w_conv.shape[0]), (0, 0))
                       ).reshape(HIST_ROWS, -1, LANES).swapaxes(0, 1),
        b_conv=b_conv.reshape(-1, 1, LANES),
        g_conv=vec(g_conv_ln), bl_conv=vec(b_conv_ln),
        wo_a=w_out[:c1].astype(BF), wo_b=w_out[c1:].astype(BF), g1=vec(g_ln1), b1=vec(b_ln1),
        w_mq=w_mq.astype(BF), w_mo=w_mo.astype(BF), g2=vec(g_ln2), b2=vec(b_ln2),
        w_up=w_up.astype(BF), w_fc=w_ffn_conv, b_fc=vec(b_ffn_conv), w_down=w_down.astype(BF),
        g3=vec(g_ln3), b3=vec(b_ln3))


def _run_layer(x, p, cfg, *, nb, conv_hist, fox_cache, mem_k, mem_v, ffn_hist):
    r, d = x.shape
    t = r // nb
    heads, dh, width = cfg["fox_heads"], cfg["fox_head_dim"], cfg["conv_width"]
    tm_in = min(r, 512)
    glu, qb, kf, kb, vf, vb, lf = _inproj(x, p["wa"], p["wg"], p["wq"], p["wk"], p["wv"], p["wf"], p["bf"],
                                           dh ** -0.5, tm_in, 512)
    logf = lf[:, :heads]

    if conv_hist is None:
        tt = min(t, 256)
        per = tt // HIST_ROWS
        hist_spec = pl.BlockSpec((HIST_ROWS, glu.shape[1]), lambda i: (jnp.maximum(i * per - 1, 0), 0))
        ca = _conv_module(glu, hist_spec, glu, p["w_conv"], p["b_conv"], p["g_conv"], p["bl_conv"], True, tt, width)
        new_conv = glu[r - (width - 1):][None]
    else:
        hist = jnp.pad(conv_hist, ((0, 0), (HIST_ROWS - (width - 1), 0), (0, 0)))
        hist_spec = pl.BlockSpec((None, HIST_ROWS, glu.shape[1]), lambda i: (i, 0, 0))
        ca = _conv_module(hist, hist_spec, glu, p["w_conv"], p["b_conv"], p["g_conv"], p["bl_conv"], False, t, width)
        conv_in = jnp.concatenate([conv_hist, glu.reshape(nb, t, -1)], axis=1)
        new_conv = conv_in[:, -(width - 1):]

    if fox_cache is None:
        ft = _cumsum_lanes(logf.T).reshape(heads // 2, 2, t)
        ob = _flash_attention(qb, kb, vb, ft, min(t, 256), dh)
    else:
        k_hist, v_hist, lf_hist = fox_cache
        past = k_hist.shape[1]
        lf_all = jnp.concatenate([lf_hist, logf.reshape(nb, t, heads)], axis=1)
        lp = -(-(past + t) // LANES) * LANES
        lf_all = jnp.pad(lf_all, ((0, 0), (0, lp - past - t), (0, 0)))
        ft = _cumsum_lanes(jnp.swapaxes(lf_all, 1, 2).reshape(nb * heads, lp)).reshape(nb, heads // 2, 2, lp)
        ob = _decode_attention(qb, kb, vb, k_hist.reshape(nb, past, -1), v_hist.reshape(nb, past, -1), ft, t, dh)

    tm = min(r, 256)
    x1, x1b = _mm_ln([ca, ob], [p["wo_a"], p["wo_b"]], x, p["g1"], p["b1"], cfg["alpha"], tm)
    om = _xattn(x1b, p["w_mq"], mem_k, mem_v, cfg["mem_heads"], t if mem_k.shape[0] > 1 else tm)
    x2, x2b = _mm_ln([om], [p["w_mo"]], x1, p["g2"], p["b2"], cfg["alpha"], tm)

    dff2 = p["w_up"].shape[1]
    if ffn_hist is None:
        seg = min(r, 512)
        hist = jnp.zeros((1, FFN_PAD, dff2), F32)
        y, tg, tv = _ffn(x2b, x2, hist, p["w_up"], p["w_fc"], p["b_fc"], p["w_down"], p["g3"], p["b3"],
                         cfg["alpha"], 1, seg, True, 512)
    else:
        hist = jnp.pad(ffn_hist, ((0, 0), (FFN_PAD - ffn_hist.shape[1], 0), (0, 0)))
        y, tg, tv = _ffn(x2b, x2, hist, p["w_up"], p["w_fc"], p["b_fc"], p["w_down"], p["g3"], p["b3"],
                         cfg["alpha"], nb, t, False, 512)
    nhist = cfg["ffn_width"] - 1
    new_ffn = jnp.concatenate([tg, tv], axis=-1)[:, FFN_PAD - nhist:]
    k_new = kf.reshape(nb, t, heads, dh)
    v_new = vf.reshape(nb, t, heads, dh)
    return y, new_conv, k_new, v_new, logf.reshape(nb, t, heads), new_ffn


def kernel(x_prompt, x_sample, cache_conv, cache_fox_k, cache_fox_v, cache_fox_logf, cache_mem_k, cache_mem_v, cache_ffn, mem_prompt, w_in, b_forget, w_conv, b_conv, g_conv_ln, b_conv_ln, w_out, g_ln1, b_ln1, w_mq, w_mk, w_mv, w_mo, g_ln2, b_ln2, w_up, w_ffn_conv, b_ffn_conv, w_down, g_ln3, b_ln3):
    depth = w_in.shape[0]
    bp, tp, d = x_prompt.shape
    bs, ts, _ = x_sample.shape
    assert bp == 1, "prompt path handles a single sequence"
    heads, dh = cache_fox_k.shape[3], cache_fox_k.shape[4]
    n_mem, mem_heads = cache_mem_k.shape[2], cache_mem_k.shape[3]
    conv_ch = w_conv.shape[2]
    cfg = dict(fox_heads=heads, fox_head_dim=dh, conv_width=w_conv.shape[1], mem_heads=mem_heads,
               ffn_width=w_ffn_conv.shape[1], alpha=float((2 * depth) ** 0.25))

    yp = x_prompt.reshape(bp * tp, d)
    ys = x_sample.reshape(bs * ts, d)
    outs = [[] for _ in range(12)]
    for l in range(depth):
        p = _layer_weights(w_in[l], b_forget[l], w_conv[l], b_conv[l], g_conv_ln[l], b_conv_ln[l], w_out[l],
                           g_ln1[l], b_ln1[l], w_mq[l], w_mo[l], g_ln2[l], b_ln2[l], w_up[l], w_ffn_conv[l],
                           b_ffn_conv[l], w_down[l], g_ln3[l], b_ln3[l], conv_ch, heads * dh)
        mem2d = mem_prompt.reshape(bp * n_mem, d)
        mk = _matmul(mem2d, w_mk[l].astype(BF), n_mem, 512)
        mv = _matmul(mem2d, w_mv[l].astype(BF), n_mem, 512)
        yp, c_, k_, v_, lf_, f_ = _run_layer(yp, p, cfg, nb=bp, conv_hist=None, fox_cache=None,
                                             mem_k=mk[None], mem_v=mv[None], ffn_hist=None)
        for o, a in zip(outs[:7], (c_, k_, v_, lf_, mk.reshape(bp, n_mem, mem_heads, -1),
                                   mv.reshape(bp, n_mem, mem_heads, -1), f_)):
            o.append(a)
        ys, c_, k_, v_, lf_, f_ = _run_layer(
            ys, p, cfg, nb=bs, conv_hist=cache_conv[l],
            fox_cache=(cache_fox_k[l], cache_fox_v[l], cache_fox_logf[l]),
            mem_k=cache_mem_k[l].reshape(bs, n_mem, d), mem_v=cache_mem_v[l].reshape(bs, n_mem, d),
            ffn_hist=cache_ffn[l])
        for o, a in zip(outs[7:], (c_, k_, v_, lf_, f_)):
            o.append(a)
    stacked = [jnp.stack(o) for o in outs]
    return (yp.reshape(bp, tp, d), ys.reshape(bs, ts, d), *stacked)
```

```python
import functools

import jax
import jax.numpy as jnp
from jax import lax
from jax.experimental import pallas as pl
from jax.experimental.pallas import tpu as pltpu

BF = jnp.bfloat16
F32 = jnp.float32

LANES = 128
HIST_ROWS = 32
FFN_PAD = 8
LN_EPS = 1e-5
NEG_BIG = -1e30
MIB = 1024 * 1024


def _cparams(n_axes, vmem_mib):
    return pltpu.CompilerParams(dimension_semantics=("arbitrary",) * n_axes,
                                vmem_limit_bytes=vmem_mib * MIB)


def _dot(a, b):
    return jnp.dot(a, b, preferred_element_type=F32)


def _dot_t(a, b):
    return lax.dot_general(a, b, (((1,), (1,)), ((), ())), preferred_element_type=F32)


def _layernorm(xf, g, b):
    mu = jnp.mean(xf, axis=-1, keepdims=True)
    xc = xf - mu
    var = jnp.mean(xc * xc, axis=-1, keepdims=True)
    return xc * lax.rsqrt(var + LN_EPS) * g + b


def _log_sigmoid(z):
    return jnp.minimum(z, 0.0) - jnp.log1p(jnp.exp(-jnp.abs(z)))


def _mm_kernel(x_ref, w_ref, o_ref):
    o_ref[...] = _dot(x_ref[...].astype(BF), w_ref[...])


def _matmul(x, w, tm, tn):
    m, k = x.shape
    n = w.shape[1]
    return pl.pallas_call(
        _mm_kernel,
        grid=(m // tm, n // tn),
        in_specs=[pl.BlockSpec((tm, k), lambda i, j: (i, 0)),
                  pl.BlockSpec((k, tn), lambda i, j: (0, j))],
        out_specs=pl.BlockSpec((tm, tn), lambda i, j: (i, j)),
        out_shape=jax.ShapeDtypeStruct((m, n), F32),
        compiler_params=_cparams(2, 40),
        name="mem_proj",
    )(x, w)


def _inproj_kernel(x_ref, wa_ref, wg_ref, wq_ref, wk_ref, wv_ref, wf_ref, bf_ref,
                   glu_ref, qb_ref, kf_ref, kb_ref, vf_ref, vb_ref, lf_ref, xb_ref, *, q_scale):
    @pl.when(pl.program_id(1) == 0)
    def _():
        xb0 = x_ref[...].astype(BF)
        xb_ref[...] = xb0
        lf_ref[...] = _log_sigmoid(_dot(xb0, wf_ref[...]) + bf_ref[...])

    xb = xb_ref[...]
    glu_ref[...] = _dot(xb, wa_ref[...]) * jax.nn.sigmoid(_dot(xb, wg_ref[...]))
    qb_ref[...] = (_dot(xb, wq_ref[...]) * q_scale).astype(BF)
    k = _dot(xb, wk_ref[...])
    kf_ref[...] = k
    kb_ref[...] = k.astype(BF)
    v = _dot(xb, wv_ref[...])
    vf_ref[...] = v
    vb_ref[...] = v.astype(BF)


def _inproj(x, wa, wg, wq, wk, wv, wf, bfg, q_scale, tm, tn):
    r, d = x.shape
    c = wa.shape[1]
    wspec = pl.BlockSpec((d, tn), lambda i, j: (0, j))
    ospec = pl.BlockSpec((tm, tn), lambda i, j: (i, j))
    f32o = jax.ShapeDtypeStruct((r, c), F32)
    bf16o = jax.ShapeDtypeStruct((r, c), BF)
    return pl.pallas_call(
        functools.partial(_inproj_kernel, q_scale=q_scale),
        grid=(r // tm, c // tn),
        in_specs=[pl.BlockSpec((tm, d), lambda i, j: (i, 0)),
                  wspec, wspec, wspec, wspec, wspec,
                  pl.BlockSpec((d, LANES), lambda i, j: (0, 0)),
                  pl.BlockSpec((1, LANES), lambda i, j: (0, 0))],
        out_specs=[ospec, ospec, ospec, ospec, ospec, ospec,
                   pl.BlockSpec((tm, LANES), lambda i, j: (i, 0))],
        out_shape=[f32o, bf16o, f32o, bf16o, f32o, bf16o,
                   jax.ShapeDtypeStruct((r, LANES), F32)],
        scratch_shapes=[pltpu.VMEM((tm, d), BF)],
        compiler_params=_cparams(2, 56),
        name="in_proj",
    )(x, wa, wg, wq, wk, wv, wf, bfg)


def _conv_kernel(hist_ref, cur_ref, w_ref, b_ref, g_ref, bl_ref, o_ref, xin_ref, y_ref,
                 *, zero_first, tt, rc, width):
    hist = hist_ref[...]
    if zero_first:
        hist = jnp.where(pl.program_id(0) == 0, 0.0, hist)
    nslab = cur_ref.shape[1] // LANES
    for c in range(nslab):
        cs = slice(c * LANES, (c + 1) * LANES)
        xin_ref[c, 0:HIST_ROWS, :] = hist[:, cs]
        xin_ref[c, HIST_ROWS:HIST_ROWS + tt, :] = cur_ref[:, cs]
    first = HIST_ROWS - (width - 1)

    def slab(c, carry):
        wc = w_ref[c]
        bc = b_ref[c]
        for r0 in range(0, tt, rc):
            acc = wc[0:1, :] * xin_ref[c, r0 + first:r0 + first + rc, :]
            for j in range(1, width):
                acc = acc + wc[j:j + 1, :] * xin_ref[c, r0 + first + j:r0 + first + j + rc, :]
            y_ref[c, r0:r0 + rc, :] = acc + bc
        return carry

    lax.fori_loop(0, nslab, slab, 0)
    y = jnp.concatenate([y_ref[c] for c in range(nslab)], axis=1)
    yn = _layernorm(y, g_ref[...], bl_ref[...])
    o_ref[...] = (yn * jax.nn.sigmoid(yn)).astype(BF)


def _conv_module(hist_arr, hist_spec, glu, w, b, g, bl, zero_first, tt, width):
    r, c = glu.shape
    rc = min(tt, 64)
    nslab = c // LANES
    vec = pl.BlockSpec((1, c), lambda i: (0, 0))
    return pl.pallas_call(
        functools.partial(_conv_kernel, zero_first=zero_first, tt=tt, rc=rc, width=width),
        grid=(r // tt,),
        in_specs=[hist_spec,
                  pl.BlockSpec((tt, c), lambda i: (i, 0)),
                  pl.BlockSpec(w.shape, lambda i: (0, 0, 0)),
                  pl.BlockSpec(b.shape, lambda i: (0, 0, 0)),
                  vec, vec],
        out_specs=pl.BlockSpec((tt, c), lambda i: (i, 0)),
        out_shape=jax.ShapeDtypeStruct((r, c), BF),
        scratch_shapes=[pltpu.VMEM((nslab, HIST_ROWS + tt, LANES), F32),
                        pltpu.VMEM((nslab, tt, LANES), F32)],
        compiler_params=_cparams(1, 32),
        name="conv_module",
    )(hist_arr, glu, w, b, g, bl)


def _cumsum_kernel(x_ref, o_ref):
    x = x_ref[...]
    n = x.shape[1]
    idx = lax.broadcasted_iota(jnp.int32, x.shape, 1)
    s = 1
    while s < n:
        x = x + jnp.where(idx >= s, pltpu.roll(x, s, axis=1), 0.0)
        s *= 2
    o_ref[...] = x


def _cumsum_lanes(x):
    return pl.pallas_call(
        _cumsum_kernel,
        out_shape=jax.ShapeDtypeStruct(x.shape, F32),
        compiler_params=pltpu.CompilerParams(vmem_limit_bytes=32 * MIB),
        name="cumsum_logf",
    )(x)


def _stack_heads(q, half):
    lane = lax.broadcasted_iota(jnp.int32, q.shape, 1)
    zero = jnp.zeros_like(q)
    return jnp.concatenate([jnp.where(lane < half, q, zero), jnp.where(lane >= half, q, zero)], axis=0)


def _unstack_heads(o, t, half):
    lane = lax.broadcasted_iota(jnp.int32, (t, o.shape[1]), 1)
    return jnp.where(lane < half, o[:t], o[t:])


def _flash_kernel(q_ref, k_ref, v_ref, f_ref, o_ref, qs_ref, v0_ref, v1_ref, sa_ref, sb_ref, m_ref, acc_ref,
                  *, tq, half):
    qi = pl.program_id(1)
    lane = lax.broadcasted_iota(jnp.int32, (tq, LANES), 1)

    @pl.when(qi == 0)
    def _():
        lane_t = lax.broadcasted_iota(jnp.int32, v_ref.shape, 1)
        v = v_ref[...]
        one = jnp.ones_like(v)
        v0_ref[...] = jnp.where(lane_t < half, v, one)
        v1_ref[...] = jnp.where(lane_t >= half, v, one)

    qs_ref[...] = _stack_heads(q_ref[...], half)
    m_ref[...] = jnp.full(m_ref.shape, NEG_BIG, F32)
    acc_ref[...] = jnp.zeros(acc_ref.shape, F32)

    def scores(j, s_ref):
        start = pl.multiple_of(j * tq, tq)
        s = _dot_t(qs_ref[...], k_ref[pl.ds(start, tq), :])
        s_ref[...] = jnp.concatenate([s[:tq] - f_ref[0:1, pl.ds(start, tq)],
                                      s[tq:] - f_ref[1:2, pl.ds(start, tq)]], axis=0)

    def consume(j, s_ref, masked):
        start = pl.multiple_of(j * tq, tq)
        s = s_ref[...]
        if masked:
            row = lax.broadcasted_iota(jnp.int32, (tq, tq), 0)
            col = lax.broadcasted_iota(jnp.int32, (tq, tq), 1)
            vis = col <= row
            s = jnp.concatenate([jnp.where(vis, s[:tq], NEG_BIG), jnp.where(vis, s[tq:], NEG_BIG)], axis=0)
        m_prev = m_ref[...]
        m_new = jnp.maximum(m_prev, jnp.max(s, axis=1, keepdims=True))
        alpha = jnp.exp(m_prev - m_new)
        p = jnp.exp(s - jnp.concatenate([m_new] * (tq // LANES), axis=1)).astype(BF)
        pv = jnp.concatenate([_dot(p[:tq], v0_ref[pl.ds(start, tq), :]),
                              _dot(p[tq:], v1_ref[pl.ds(start, tq), :])], axis=0)
        acc_ref[...] = alpha * acc_ref[...] + pv
        m_ref[...] = m_new

    scores(0, sa_ref)

    def pair(i, carry):
        j = 2 * i
        scores(j + 1, sb_ref)
        consume(j, sa_ref, False)
        scores(j + 2, sa_ref)
        consume(j + 1, sb_ref, False)
        return carry

    lax.fori_loop(0, qi // 2, pair, 0)
    odd = (qi & 1) == 1

    @pl.when(odd)
    def _():
        scores(qi, sb_ref)
        consume(qi - 1, sa_ref, False)
        consume(qi, sb_ref, True)

    @pl.when(jnp.logical_not(odd))
    def _():
        consume(qi, sa_ref, True)

    acc = acc_ref[...]
    out = acc / pltpu.roll(acc, half, axis=1)
    o_ref[...] = jnp.where(lane < half, out[:tq], out[tq:]).astype(BF)


def _flash_attention(qb, kb, vb, ft, tq, head_dim):
    t, c = qb.shape
    npair = c // LANES
    return pl.pallas_call(
        functools.partial(_flash_kernel, tq=tq, half=head_dim),
        grid=(npair, t // tq),
        in_specs=[pl.BlockSpec((tq, LANES), lambda hp, qi: (qi, hp)),
                  pl.BlockSpec((t, LANES), lambda hp, qi: (0, hp)),
                  pl.BlockSpec((t, LANES), lambda hp, qi: (0, hp)),
                  pl.BlockSpec((None, 2, t), lambda hp, qi: (hp, 0, 0))],
        out_specs=pl.BlockSpec((tq, LANES), lambda hp, qi: (qi, hp)),
        out_shape=jax.ShapeDtypeStruct((t, c), BF),
        scratch_shapes=[pltpu.VMEM((2 * tq, LANES), BF),
                        pltpu.VMEM((t, LANES), BF),
                        pltpu.VMEM((t, LANES), BF),
                        pltpu.VMEM((2 * tq, tq), F32),
                        pltpu.VMEM((2 * tq, tq), F32),
                        pltpu.VMEM((2 * tq, LANES), F32),
                        pltpu.VMEM((2 * tq, LANES), F32)],
        compiler_params=_cparams(2, 48),
        name="fox_flash",
    )(qb, kb, vb, ft)


def _decode_kernel(q_ref, kn_ref, vn_ref, kc_ref, vc_ref, f_ref, o_ref, *, past, tn, half):
    qs = _stack_heads(q_ref[...], half)
    kc = kc_ref[...].astype(BF)
    vc = vc_ref[...].astype(BF)
    sc = _dot_t(qs, kc)
    sn = _dot_t(qs, kn_ref[...])
    sc = jnp.concatenate([sc[:tn] - f_ref[0:1, 0:past], sc[tn:] - f_ref[1:2, 0:past]], axis=0)
    row = lax.broadcasted_iota(jnp.int32, (tn, tn), 0)
    col = lax.broadcasted_iota(jnp.int32, (tn, tn), 1)
    vis = col <= row
    sn0 = jnp.where(vis, sn[:tn] - f_ref[0:1, past:past + tn], NEG_BIG)
    sn1 = jnp.where(vis, sn[tn:] - f_ref[1:2, past:past + tn], NEG_BIG)
    sn = jnp.concatenate([sn0, sn1], axis=0)
    m = jnp.maximum(jnp.max(sc, axis=1, keepdims=True), jnp.max(sn, axis=1, keepdims=True))
    pc = jnp.exp(sc - m)
    pn = jnp.exp(sn - m)
    l = jnp.sum(pc, axis=1, keepdims=True) + jnp.sum(pn, axis=1, keepdims=True)
    o = (_dot(pc.astype(BF), vc) + _dot(pn.astype(BF), vn_ref[...])) / l
    o_ref[...] = _unstack_heads(o, tn, half).astype(BF)


def _decode_attention(qb, kb, vb, kcache, vcache, ft, tn, head_dim):
    r, c = qb.shape
    nb, past, _ = kcache.shape
    npair = c // LANES
    lp = ft.shape[-1]
    new = pl.BlockSpec((tn, LANES), lambda b, hp: (b, hp))
    cache = pl.BlockSpec((None, past, LANES), lambda b, hp: (b, 0, hp))
    return pl.pallas_call(
        functools.partial(_decode_kernel, past=past, tn=tn, half=head_dim),
        grid=(nb, npair),
        in_specs=[new, new, new, cache, cache,
                  pl.BlockSpec((None, None, 2, lp), lambda b, hp: (b, hp, 0, 0))],
        out_specs=new,
        out_shape=jax.ShapeDtypeStruct((r, c), BF),
        compiler_params=_cparams(2, 32),
        name="fox_decode",
    )(qb, kb, vb, kcache, vcache, ft)


def _mm_ln_kernel(*refs, n_lhs, alpha):
    lhs = refs[:n_lhs]
    ws = refs[n_lhs:2 * n_lhs]
    r_ref, g_ref, b_ref, of_ref, ob_ref = refs[2 * n_lhs:]
    acc = _dot(lhs[0][...], ws[0][...])
    for a_ref, w_ref in zip(lhs[1:], ws[1:]):
        acc = acc + _dot(a_ref[...], w_ref[...])
    y = _layernorm(alpha * r_ref[...] + acc, g_ref[...], b_ref[...])
    of_ref[...] = y
    ob_ref[...] = y.astype(BF)


def _mm_ln(lhs_list, w_list, resid, g, b, alpha, tm):
    r, d = resid.shape
    n = len(lhs_list)
    row = lambda i: (i, 0)
    const = lambda i: (0, 0)
    return pl.pallas_call(
        functools.partial(_mm_ln_kernel, n_lhs=n, alpha=alpha),
        grid=(r // tm,),
        in_specs=([pl.BlockSpec((tm, a.shape[1]), row) for a in lhs_list]
                  + [pl.BlockSpec(w.shape, const) for w in w_list]
                  + [pl.BlockSpec((tm, d), row), pl.BlockSpec((1, d), const), pl.BlockSpec((1, d), const)]),
        out_specs=[pl.BlockSpec((tm, d), row), pl.BlockSpec((tm, d), row)],
        out_shape=[jax.ShapeDtypeStruct((r, d), F32), jax.ShapeDtypeStruct((r, d), BF)],
        compiler_params=_cparams(1, 48),
        name="proj_ln",
    )(*lhs_list, *w_list, resid, g, b)


def _xattn_kernel(x_ref, wq_ref, mk_ref, mv_ref, o_ref, *, heads, scale):
    x = x_ref[...]
    dh = x.shape[1] // heads
    for h in range(heads):
        hs = slice(h * dh, (h + 1) * dh)
        qm = _dot(x, wq_ref[:, hs]).astype(BF)
        s = _dot_t(qm, mk_ref[:, hs].astype(BF)) * scale
        p = jnp.exp(s - jnp.max(s, axis=1, keepdims=True))
        l = jnp.sum(p, axis=1, keepdims=True)
        o_ref[:, hs] = (_dot(p.astype(BF), mv_ref[:, hs].astype(BF)) / l).astype(BF)


def _xattn(xb, wq, mk, mv, heads, tm):
    r, d = xb.shape
    nmb, n_mem, _ = mk.shape
    mem_idx = (lambda i: (i, 0, 0)) if nmb > 1 else (lambda i: (0, 0, 0))
    return pl.pallas_call(
        functools.partial(_xattn_kernel, heads=heads, scale=(d // heads) ** -0.5),
        grid=(r // tm,),
        in_specs=[pl.BlockSpec((tm, d), lambda i: (i, 0)),
                  pl.BlockSpec((d, d), lambda i: (0, 0)),
                  pl.BlockSpec((None, n_mem, d), mem_idx),
                  pl.BlockSpec((None, n_mem, d), mem_idx)],
        out_specs=pl.BlockSpec((tm, d), lambda i: (i, 0)),
        out_shape=jax.ShapeDtypeStruct((r, d), BF),
        compiler_params=_cparams(1, 48),
        name="mem_xattn",
    )(xb, wq, mk, mv)


def _ffn_kernel(xb_ref, xf_ref, hg_ref, hv_ref, wug_ref, wuv_ref, wcg_ref, wcv_ref, bcg_ref, bcv_ref,
                wd_ref, g_ref, b_ref, o_ref, tg_ref, tv_ref, acc_ref, hbuf_ref, carry_ref,
                *, nseg, seg, use_carry, alpha):
    i = pl.program_id(0)
    f = pl.program_id(1)

    @pl.when(f == 0)
    def _():
        acc_ref[...] = jnp.zeros(acc_ref.shape, F32)

    xb = xb_ref[...]

    def conv_half(wu_ref, wc_ref, bc_ref, hist_ref, tail_ref, which):
        h = _dot(xb, wu_ref[...])
        w = wc_ref[...]
        outs = []
        for b in range(nseg):
            base = b * (seg + FFN_PAD)
            hb = h[b * seg:(b + 1) * seg]
            if use_carry:
                @pl.when(i == 0)
                def _(b=b):
                    carry_ref[which, f] = hist_ref[b]

                hist = carry_ref[which, f]
            else:
                hist = hist_ref[b]
            hbuf_ref[base:base + FFN_PAD, :] = hist
            hbuf_ref[base + FFN_PAD:base + FFN_PAD + seg, :] = hb
            p1 = hbuf_ref[base + FFN_PAD - 1:base + FFN_PAD - 1 + seg, :]
            p2 = hbuf_ref[base + FFN_PAD - 2:base + FFN_PAD - 2 + seg, :]
            outs.append(w[0:1, :] * p2 + w[1:2, :] * p1 + w[2:3, :] * hb + bc_ref[...])
            tail = hb[seg - FFN_PAD:seg]
            tail_ref[b] = tail
            if use_carry:
                carry_ref[which, f] = tail
        return outs[0] if nseg == 1 else jnp.concatenate(outs, axis=0)

    cg = conv_half(wug_ref, wcg_ref, bcg_ref, hg_ref, tg_ref, 0)
    cv = conv_half(wuv_ref, wcv_ref, bcv_ref, hv_ref, tv_ref, 1)
    act = (cg * jax.nn.sigmoid(cg) * cv).astype(BF)
    acc_ref[...] += _dot(act, wd_ref[...])

    @pl.when(f == pl.num_programs(1) - 1)
    def _():
        o_ref[...] = _layernorm(alpha * xf_ref[...] + acc_ref[...], g_ref[...], b_ref[...])


def _ffn(xb, xf, hist, wu, wc, bc, wd, g, b, alpha, nseg, seg, use_carry, tf):
    r, d = xf.shape
    tm = nseg * seg
    dff = wd.shape[0]
    nf = dff // tf
    row = lambda i, f: (i, 0)
    const = lambda i, f: (0, 0)
    lo = lambda i, f: (0, f)
    hi = lambda i, f: (0, nf + f)
    lo3 = lambda i, f: (0, 0, f)
    hi3 = lambda i, f: (0, 0, nf + f)
    ntile = r // tm
    tail = jax.ShapeDtypeStruct((ntile * nseg, FFN_PAD, dff), F32)
    tail_spec = pl.BlockSpec((nseg, FFN_PAD, tf), lambda i, f: (i, 0, f))
    y, tg, tv = pl.pallas_call(
        functools.partial(_ffn_kernel, nseg=nseg, seg=seg, use_carry=use_carry, alpha=alpha),
        grid=(r // tm, nf),
        in_specs=[pl.BlockSpec((tm, d), row), pl.BlockSpec((tm, d), row),
                  pl.BlockSpec((nseg, FFN_PAD, tf), lo3), pl.BlockSpec((nseg, FFN_PAD, tf), hi3),
                  pl.BlockSpec((d, tf), lo), pl.BlockSpec((d, tf), hi),
                  pl.BlockSpec((wc.shape[0], tf), lo), pl.BlockSpec((wc.shape[0], tf), hi),
                  pl.BlockSpec((1, tf), lo), pl.BlockSpec((1, tf), hi),
                  pl.BlockSpec((tf, d), lambda i, f: (f, 0)),
                  pl.BlockSpec((1, d), const), pl.BlockSpec((1, d), const)],
        out_specs=[pl.BlockSpec((tm, d), row), tail_spec, tail_spec],
        out_shape=[jax.ShapeDtypeStruct((r, d), F32), tail, tail],
        scratch_shapes=[pltpu.VMEM((tm, d), F32),
                        pltpu.VMEM((nseg * (seg + FFN_PAD), tf), F32),
                        pltpu.VMEM((2, nf, FFN_PAD, tf), F32)],
        compiler_params=_cparams(2, 56),
        name="conv_ffn",
    )(xb, xf, hist, hist, wu, wu, wc, wc, bc, bc, wd, g, b)
    last = (ntile - 1) * nseg
    return y, tg[last:], tv[last:]


def _pad_lanes(a, n):
    return jnp.pad(a, ((0, 0), (0, n - a.shape[1])))


def _layer_weights(w_in, b_forget, w_conv, b_conv, g_conv_ln, b_conv_ln, w_out, g_ln1, b_ln1,
                   w_mq, w_mo, g_ln2, b_ln2, w_up, w_ffn_conv, b_ffn_conv, w_down, g_ln3, b_ln3,
                   conv_ch, fox_w):
    c1, c2 = conv_ch, 2 * conv_ch
    c3, c4, c5 = c2 + fox_w, c2 + 2 * fox_w, c2 + 3 * fox_w
    wb = w_in.astype(BF)
    vec = lambda a: a[None, :]
    return dict(
        wa=wb[:, :c1], wg=wb[:, c1:c2], wq=wb[:, c2:c3], wk=wb[:, c3:c4], wv=wb[:, c4:c5],
        wf=_pad_lanes(wb[:, c5:], LANES), bf=_pad_lanes(vec(b_forget), LANES),
        w_conv=jnp.pad(w_conv, ((0, HIST_ROWS - w_conv.shape[0]), (0, 0))
                       ).reshape(HIST_ROWS, -1, LANES).swapaxes(0, 1),
        b_conv=b_conv.reshape(-1, 1, LANES),
        g_conv=vec(g_conv_ln), bl_conv=vec(b_conv_ln),
        wo_a=w_out[:c1].astype(BF), wo_b=w_out[c1:].astype(BF), g1=vec(g_ln1), b1=vec(b_ln1),
        w_mq=w_mq.astype(BF), w_mo=w_mo.astype(BF), g2=vec(g_ln2), b2=vec(b_ln2),
        w_up=w_up.astype(BF), w_fc=w_ffn_conv, b_fc=vec(b_ffn_conv), w_down=w_down.astype(BF),
        g3=vec(g_ln3), b3=vec(b_ln3))


def _run_layer(x, p, cfg, *, nb, conv_hist, fox_cache, mem_k, mem_v, ffn_hist):
    r, d = x.shape
    t = r // nb
    heads, dh, width = cfg["fox_heads"], cfg["fox_head_dim"], cfg["conv_width"]
    tm_in = min(r, 512)
    glu, qb, kf, kb, vf, vb, lf = _inproj(x, p["wa"], p["wg"], p["wq"], p["wk"], p["wv"], p["wf"], p["bf"],
                                           dh ** -0.5, tm_in, 512)
    logf = lf[:, :heads]

    if conv_hist is None:
        tt = min(t, 256)
        per = tt // HIST_ROWS
        hist_spec = pl.BlockSpec((HIST_ROWS, glu.shape[1]), lambda i: (jnp.maximum(i * per - 1, 0), 0))
        ca = _conv_module(glu, hist_spec, glu, p["w_conv"], p["b_conv"], p["g_conv"], p["bl_conv"], True, tt, width)
        new_conv = glu[r - (width - 1):][None]
    else:
        hist = jnp.pad(conv_hist, ((0, 0), (HIST_ROWS - (width - 1), 0), (0, 0)))
        hist_spec = pl.BlockSpec((None, HIST_ROWS, glu.shape[1]), lambda i: (i, 0, 0))
        ca = _conv_module(hist, hist_spec, glu, p["w_conv"], p["b_conv"], p["g_conv"], p["bl_conv"], False, t, width)
        conv_in = jnp.concatenate([conv_hist, glu.reshape(nb, t, -1)], axis=1)
        new_conv = conv_in[:, -(width - 1):]

    if fox_cache is None:
        ft = _cumsum_lanes(logf.T).reshape(heads // 2, 2, t)
        ob = _flash_attention(qb, kb, vb, ft, min(t, 512), dh)
    else:
        k_hist, v_hist, lf_hist = fox_cache
        past = k_hist.shape[1]
        lf_all = jnp.concatenate([lf_hist, logf.reshape(nb, t, heads)], axis=1)
        lp = -(-(past + t) // LANES) * LANES
        lf_all = jnp.pad(lf_all, ((0, 0), (0, lp - past - t), (0, 0)))
        ft = _cumsum_lanes(jnp.swapaxes(lf_all, 1, 2).reshape(nb * heads, lp)).reshape(nb, heads // 2, 2, lp)
        ob = _decode_attention(qb, kb, vb, k_hist.reshape(nb, past, -1), v_hist.reshape(nb, past, -1), ft, t, dh)

    tm = min(r, 256)
    x1, x1b = _mm_ln([ca, ob], [p["wo_a"], p["wo_b"]], x, p["g1"], p["b1"], cfg["alpha"], tm)
    om = _xattn(x1b, p["w_mq"], mem_k, mem_v, cfg["mem_heads"], t if mem_k.shape[0] > 1 else tm)
    x2, x2b = _mm_ln([om], [p["w_mo"]], x1, p["g2"], p["b2"], cfg["alpha"], tm)

    dff2 = p["w_up"].shape[1]
    if ffn_hist is None:
        seg = min(r, 512)
        hist = jnp.zeros((1, FFN_PAD, dff2), F32)
        y, tg, tv = _ffn(x2b, x2, hist, p["w_up"], p["w_fc"], p["b_fc"], p["w_down"], p["g3"], p["b3"],
                         cfg["alpha"], 1, seg, True, 512)
    else:
        hist = jnp.pad(ffn_hist, ((0, 0), (FFN_PAD - ffn_hist.shape[1], 0), (0, 0)))
        y, tg, tv = _ffn(x2b, x2, hist, p["w_up"], p["w_fc"], p["b_fc"], p["w_down"], p["g3"], p["b3"],
                         cfg["alpha"], nb, t, False, 512)
    nhist = cfg["ffn_width"] - 1
    new_ffn = jnp.concatenate([tg, tv], axis=-1)[:, FFN_PAD - nhist:]
    k_new = kf.reshape(nb, t, heads, dh)
    v_new = vf.reshape(nb, t, heads, dh)
    return y, new_conv, k_new, v_new, logf.reshape(nb, t, heads), new_ffn


def kernel(x_prompt, x_sample, cache_conv, cache_fox_k, cache_fox_v, cache_fox_logf, cache_mem_k, cache_mem_v, cache_ffn, mem_prompt, w_in, b_forget, w_conv, b_conv, g_conv_ln, b_conv_ln, w_out, g_ln1, b_ln1, w_mq, w_mk, w_mv, w_mo, g_ln2, b_ln2, w_up, w_ffn_conv, b_ffn_conv, w_down, g_ln3, b_ln3):
    depth = w_in.shape[0]
    bp, tp, d = x_prompt.shape
    bs, ts, _ = x_sample.shape
    assert bp == 1, "prompt path handles a single sequence"
    heads, dh = cache_fox_k.shape[3], cache_fox_k.shape[4]
    n_mem, mem_heads = cache_mem_k.shape[2], cache_mem_k.shape[3]
    conv_ch = w_conv.shape[2]
    cfg = dict(fox_heads=heads, fox_head_dim=dh, conv_width=w_conv.shape[1], mem_heads=mem_heads,
               ffn_width=w_ffn_conv.shape[1], alpha=float((2 * depth) ** 0.25))

    yp = x_prompt.reshape(bp * tp, d)
    ys = x_sample.reshape(bs * ts, d)
    outs = [[] for _ in range(12)]
    for l in range(depth):
        p = _layer_weights(w_in[l], b_forget[l], w_conv[l], b_conv[l], g_conv_ln[l], b_conv_ln[l], w_out[l],
                           g_ln1[l], b_ln1[l], w_mq[l], w_mo[l], g_ln2[l], b_ln2[l], w_up[l], w_ffn_conv[l],
                           b_ffn_conv[l], w_down[l], g_ln3[l], b_ln3[l], conv_ch, heads * dh)
        mem2d = mem_prompt.reshape(bp * n_mem, d)
        mk = _matmul(mem2d, w_mk[l].astype(BF), n_mem, 512)
        mv = _matmul(mem2d, w_mv[l].astype(BF), n_mem, 512)
        yp, c_, k_, v_, lf_, f_ = _run_layer(yp, p, cfg, nb=bp, conv_hist=None, fox_cache=None,
                                             mem_k=mk[None], mem_v=mv[None], ffn_hist=None)
        for o, a in zip(outs[:7], (c_, k_, v_, lf_, mk.reshape(bp, n_mem, mem_heads, -1),
                                   mv.reshape(bp, n_mem, mem_heads, -1), f_)):
            o.append(a)
        ys, c_, k_, v_, lf_, f_ = _run_layer(
            ys, p, cfg, nb=bs, conv_hist=cache_conv[l],
            fox_cache=(cache_fox_k[l], cache_fox_v[l], cache_fox_logf[l]),
            mem_k=cache_mem_k[l].reshape(bs, n_mem, d), mem_v=cache_mem_v[l].reshape(bs, n_mem, d),
            ffn_hist=cache_ffn[l])
        for o, a in zip(outs[7:], (c_, k_, v_, lf_, f_)):
            o.append(a)
    stacked = [jnp.stack(o) for o in outs]
    return (yp.reshape(bp, tp, d), ys.reshape(bs, ts, d), *stacked)
```

```python
import functools

import jax
import jax.numpy as jnp
from jax import lax
from jax.experimental import pallas as pl
from jax.experimental.pallas import tpu as pltpu

BF = jnp.bfloat16
F32 = jnp.float32

LANES = 128
HIST_ROWS = 32
FFN_PAD = 8
LN_EPS = 1e-5
NEG_BIG = -1e30
LOG2E = 1.4426950408889634
MIB = 1024 * 1024


def _cparams(n_axes, vmem_mib, flags=None):
    return pltpu.CompilerParams(dimension_semantics=("arbitrary",) * n_axes,
                                vmem_limit_bytes=vmem_mib * MIB, flags=flags)


def _dot(a, b):
    return jnp.dot(a, b, preferred_element_type=F32)


def _dot_t(a, b):
    return lax.dot_general(a, b, (((1,), (1,)), ((), ())), preferred_element_type=F32)


def _layernorm(xf, g, b):
    mu = jnp.mean(xf, axis=-1, keepdims=True)
    xc = xf - mu
    var = jnp.mean(xc * xc, axis=-1, keepdims=True)
    return xc * lax.rsqrt(var + LN_EPS) * g + b


def _log_sigmoid(z):
    return jnp.minimum(z, 0.0) - jnp.log1p(jnp.exp(-jnp.abs(z)))


def _mm_kernel(x_ref, w_ref, o_ref):
    o_ref[...] = _dot(x_ref[...].astype(BF), w_ref[...])


def _matmul(x, w, tm, tn):
    m, k = x.shape
    n = w.shape[1]
    return pl.pallas_call(
        _mm_kernel,
        grid=(m // tm, n // tn),
        in_specs=[pl.BlockSpec((tm, k), lambda i, j: (i, 0)),
                  pl.BlockSpec((k, tn), lambda i, j: (0, j))],
        out_specs=pl.BlockSpec((tm, tn), lambda i, j: (i, j)),
        out_shape=jax.ShapeDtypeStruct((m, n), F32),
        compiler_params=_cparams(2, 40),
        name="mem_proj",
    )(x, w)


def _inproj_kernel(x_ref, wa_ref, wg_ref, wq_ref, wk_ref, wv_ref, wf_ref, bf_ref,
                   glu_ref, qb_ref, kf_ref, kb_ref, vf_ref, vb_ref, lf_ref, xb_ref, *, q_scale):
    @pl.when(pl.program_id(1) == 0)
    def _():
        xb0 = x_ref[...].astype(BF)
        xb_ref[...] = xb0
        lf_ref[...] = _log_sigmoid(_dot(xb0, wf_ref[...]) + bf_ref[...])

    xb = xb_ref[...]
    glu_ref[...] = _dot(xb, wa_ref[...]) * jax.nn.sigmoid(_dot(xb, wg_ref[...]))
    qb_ref[...] = (_dot(xb, wq_ref[...]) * q_scale).astype(BF)
    k = _dot(xb, wk_ref[...])
    kf_ref[...] = k
    kb_ref[...] = k.astype(BF)
    v = _dot(xb, wv_ref[...])
    vf_ref[...] = v
    vb_ref[...] = v.astype(BF)


def _inproj(x, w, c, wf, bfg, q_scale, tm, tn):
    r, d = x.shape
    per = c // tn
    wspecs = [pl.BlockSpec((d, tn), lambda i, j, g=g: (0, g * per + j)) for g in range(5)]
    ospec = pl.BlockSpec((tm, tn), lambda i, j: (i, j))
    f32o = jax.ShapeDtypeStruct((r, c), F32)
    bf16o = jax.ShapeDtypeStruct((r, c), BF)
    return pl.pallas_call(
        functools.partial(_inproj_kernel, q_scale=q_scale),
        grid=(r // tm, c // tn),
        in_specs=[pl.BlockSpec((tm, d), lambda i, j: (i, 0)),
                  *wspecs,
                  pl.BlockSpec((d, LANES), lambda i, j: (0, 0)),
                  pl.BlockSpec((1, LANES), lambda i, j: (0, 0))],
        out_specs=[ospec, ospec, ospec, ospec, ospec, ospec,
                   pl.BlockSpec((tm, LANES), lambda i, j: (i, 0))],
        out_shape=[f32o, bf16o, f32o, bf16o, f32o, bf16o,
                   jax.ShapeDtypeStruct((r, LANES), F32)],
        scratch_shapes=[pltpu.VMEM((tm, d), BF)],
        compiler_params=_cparams(2, 56),
        name="in_proj",
    )(x, w, w, w, w, w, wf, bfg)


def _conv_kernel(hist_ref, cur_ref, w_ref, b_ref, g_ref, bl_ref, o_ref, xin_ref, y_ref,
                 *, zero_first, tt, rc, width):
    hist = hist_ref[...]
    if zero_first:
        hist = jnp.where(pl.program_id(0) == 0, 0.0, hist)
    nslab = cur_ref.shape[1] // LANES
    for c in range(nslab):
        cs = slice(c * LANES, (c + 1) * LANES)
        xin_ref[c, 0:HIST_ROWS, :] = hist[:, cs]
        xin_ref[c, HIST_ROWS:HIST_ROWS + tt, :] = cur_ref[:, cs]
    first = HIST_ROWS - (width - 1)

    def slab(c, carry):
        wc = w_ref[c]
        bc = b_ref[c]
        for r0 in range(0, tt, rc):
            acc = wc[0:1, :] * xin_ref[c, r0 + first:r0 + first + rc, :]
            for j in range(1, width):
                acc = acc + wc[j:j + 1, :] * xin_ref[c, r0 + first + j:r0 + first + j + rc, :]
            y_ref[c, r0:r0 + rc, :] = acc + bc
        return carry

    lax.fori_loop(0, nslab, slab, 0)
    y = jnp.concatenate([y_ref[c] for c in range(nslab)], axis=1)
    yn = _layernorm(y, g_ref[...], bl_ref[...])
    o_ref[...] = (yn * jax.nn.sigmoid(yn)).astype(BF)


def _conv_module(hist_arr, hist_spec, glu, w, b, g, bl, zero_first, tt, width):
    r, c = glu.shape
    rc = min(tt, 64)
    nslab = c // LANES
    vec = pl.BlockSpec((1, c), lambda i: (0, 0))
    return pl.pallas_call(
        functools.partial(_conv_kernel, zero_first=zero_first, tt=tt, rc=rc, width=width),
        grid=(r // tt,),
        in_specs=[hist_spec,
                  pl.BlockSpec((tt, c), lambda i: (i, 0)),
                  pl.BlockSpec(w.shape, lambda i: (0, 0, 0)),
                  pl.BlockSpec(b.shape, lambda i: (0, 0, 0)),
                  vec, vec],
        out_specs=pl.BlockSpec((tt, c), lambda i: (i, 0)),
        out_shape=jax.ShapeDtypeStruct((r, c), BF),
        scratch_shapes=[pltpu.VMEM((nslab, HIST_ROWS + tt, LANES), F32),
                        pltpu.VMEM((nslab, tt, LANES), F32)],
        compiler_params=_cparams(1, 32),
        name="conv_module",
    )(hist_arr, glu, w, b, g, bl)


def _cumsum_kernel(x_ref, o_ref, *, scale):
    x = x_ref[...]
    n = x.shape[1]
    idx = lax.broadcasted_iota(jnp.int32, x.shape, 1)
    s = 1
    while s < n:
        x = x + jnp.where(idx >= s, pltpu.roll(x, s, axis=1), 0.0)
        s *= 2
    o_ref[...] = x if scale == 1.0 else x * scale


def _cumsum_lanes(x, scale=1.0):
    return pl.pallas_call(
        functools.partial(_cumsum_kernel, scale=scale),
        out_shape=jax.ShapeDtypeStruct(x.shape, F32),
        compiler_params=pltpu.CompilerParams(vmem_limit_bytes=32 * MIB),
        name="cumsum_logf",
    )(x)


def _stack_heads(q, half):
    lane = lax.broadcasted_iota(jnp.int32, q.shape, 1)
    zero = jnp.zeros_like(q)
    return jnp.concatenate([jnp.where(lane < half, q, zero), jnp.where(lane >= half, q, zero)], axis=0)


def _unstack_heads(o, t, half):
    lane = lax.broadcasted_iota(jnp.int32, (t, o.shape[1]), 1)
    return jnp.where(lane < half, o[:t], o[t:])


def _flash_kernel(q_ref, k_ref, v_ref, f_ref, o_ref, qs_ref, v0_ref, v1_ref, sa_ref, sb_ref, m_ref, acc_ref,
                  *, tq, half):
    qi = pl.program_id(1)
    lane = lax.broadcasted_iota(jnp.int32, (tq, LANES), 1)

    @pl.when(qi == 0)
    def _():
        lane_t = lax.broadcasted_iota(jnp.int32, v_ref.shape, 1)
        v = v_ref[...]
        one = jnp.ones_like(v)
        v0_ref[...] = jnp.where(lane_t < half, v, one)
        v1_ref[...] = jnp.where(lane_t >= half, v, one)

    qs_ref[...] = _stack_heads(q_ref[...], half)
    m_ref[...] = jnp.full(m_ref.shape, NEG_BIG, F32)
    acc_ref[...] = jnp.zeros(acc_ref.shape, F32)

    def scores(j, s_ref):
        start = pl.multiple_of(j * tq, tq)
        s = _dot_t(qs_ref[...], k_ref[pl.ds(start, tq), :])
        s_ref[...] = jnp.concatenate([s[:tq] - f_ref[0:1, pl.ds(start, tq)],
                                      s[tq:] - f_ref[1:2, pl.ds(start, tq)]], axis=0)

    def consume(j, s_ref, masked):
        start = pl.multiple_of(j * tq, tq)
        s = s_ref[...]
        if masked:
            row = lax.broadcasted_iota(jnp.int32, (tq, tq), 0)
            col = lax.broadcasted_iota(jnp.int32, (tq, tq), 1)
            vis = col <= row
            s = jnp.concatenate([jnp.where(vis, s[:tq], NEG_BIG), jnp.where(vis, s[tq:], NEG_BIG)], axis=0)
        m_prev = m_ref[...]
        m_new = jnp.maximum(m_prev, jnp.max(s, axis=1, keepdims=True))
        alpha = jnp.exp2(m_prev - m_new)
        p = jnp.exp2(s - jnp.concatenate([m_new] * (tq // LANES), axis=1)).astype(BF)
        pv = jnp.concatenate([_dot(p[:tq], v0_ref[pl.ds(start, tq), :]),
                              _dot(p[tq:], v1_ref[pl.ds(start, tq), :])], axis=0)
        acc_ref[...] = alpha * acc_ref[...] + pv
        m_ref[...] = m_new

    scores(0, sa_ref)

    def pair(i, carry):
        j = 2 * i
        scores(j + 1, sb_ref)
        consume(j, sa_ref, False)
        scores(j + 2, sa_ref)
        consume(j + 1, sb_ref, False)
        return carry

    lax.fori_loop(0, qi // 2, pair, 0)
    odd = (qi & 1) == 1

    @pl.when(odd)
    def _():
        scores(qi, sb_ref)
        consume(qi - 1, sa_ref, False)
        consume(qi, sb_ref, True)

    @pl.when(jnp.logical_not(odd))
    def _():
        consume(qi, sa_ref, True)

    acc = acc_ref[...]
    out = acc / pltpu.roll(acc, half, axis=1)
    o_ref[...] = jnp.where(lane < half, out[:tq], out[tq:]).astype(BF)


def _flash_attention(qb, kb, vb, ft, tq, head_dim):
    t, c = qb.shape
    npair = c // LANES
    return pl.pallas_call(
        functools.partial(_flash_kernel, tq=tq, half=head_dim),
        grid=(npair, t // tq),
        in_specs=[pl.BlockSpec((tq, LANES), lambda hp, qi: (qi, hp)),
                  pl.BlockSpec((t, LANES), lambda hp, qi: (0, hp)),
                  pl.BlockSpec((t, LANES), lambda hp, qi: (0, hp)),
                  pl.BlockSpec((None, 2, t), lambda hp, qi: (hp, 0, 0))],
        out_specs=pl.BlockSpec((tq, LANES), lambda hp, qi: (qi, hp)),
        out_shape=jax.ShapeDtypeStruct((t, c), BF),
        scratch_shapes=[pltpu.VMEM((2 * tq, LANES), BF),
                        pltpu.VMEM((t, LANES), BF),
                        pltpu.VMEM((t, LANES), BF),
                        pltpu.VMEM((2 * tq, tq), F32),
                        pltpu.VMEM((2 * tq, tq), F32),
                        pltpu.VMEM((2 * tq, LANES), F32),
                        pltpu.VMEM((2 * tq, LANES), F32)],
        compiler_params=_cparams(2, 48),
        name="fox_flash",
    )(qb, kb, vb, ft)


def _decode_kernel(q_ref, kn_ref, vn_ref, kc_ref, vc_ref, f_ref, o_ref, *, past, tn, half):
    qs = _stack_heads(q_ref[...], half)
    kc = kc_ref[...].astype(BF)
    vc = vc_ref[...].astype(BF)
    sc = _dot_t(qs, kc)
    sn = _dot_t(qs, kn_ref[...])
    sc = jnp.concatenate([sc[:tn] - f_ref[0:1, 0:past], sc[tn:] - f_ref[1:2, 0:past]], axis=0)
    row = lax.broadcasted_iota(jnp.int32, (tn, tn), 0)
    col = lax.broadcasted_iota(jnp.int32, (tn, tn), 1)
    vis = col <= row
    sn0 = jnp.where(vis, sn[:tn] - f_ref[0:1, past:past + tn], NEG_BIG)
    sn1 = jnp.where(vis, sn[tn:] - f_ref[1:2, past:past + tn], NEG_BIG)
    sn = jnp.concatenate([sn0, sn1], axis=0)
    m = jnp.maximum(jnp.max(sc, axis=1, keepdims=True), jnp.max(sn, axis=1, keepdims=True))
    pc = jnp.exp(sc - m)
    pn = jnp.exp(sn - m)
    l = jnp.sum(pc, axis=1, keepdims=True) + jnp.sum(pn, axis=1, keepdims=True)
    o = (_dot(pc.astype(BF), vc) + _dot(pn.astype(BF), vn_ref[...])) / l
    o_ref[...] = _unstack_heads(o, tn, half).astype(BF)


def _decode_attention(qb, kb, vb, kcache, vcache, ft, tn, head_dim):
    r, c = qb.shape
    nb, past, _ = kcache.shape
    npair = c // LANES
    lp = ft.shape[-1]
    new = pl.BlockSpec((tn, LANES), lambda b, hp: (b, hp))
    cache = pl.BlockSpec((None, past, LANES), lambda b, hp: (b, 0, hp))
    return pl.pallas_call(
        functools.partial(_decode_kernel, past=past, tn=tn, half=head_dim),
        grid=(nb, npair),
        in_specs=[new, new, new, cache, cache,
                  pl.BlockSpec((None, None, 2, lp), lambda b, hp: (b, hp, 0, 0))],
        out_specs=new,
        out_shape=jax.ShapeDtypeStruct((r, c), BF),
        compiler_params=_cparams(2, 32),
        name="fox_decode",
    )(qb, kb, vb, kcache, vcache, ft)


def _mm_ln_kernel(*refs, n_lhs, alpha):
    lhs = refs[:n_lhs]
    ws = refs[n_lhs:2 * n_lhs]
    r_ref, g_ref, b_ref, of_ref, ob_ref = refs[2 * n_lhs:]
    tm = r_ref.shape[0]
    nsub = 2 if tm % 32 == 0 else 1
    rows = [slice(s * tm // nsub, (s + 1) * tm // nsub) for s in range(nsub)]
    accs = []
    for rs in rows:
        acc = _dot(lhs[0][rs, :], ws[0][...])
        for a_ref, w_ref in zip(lhs[1:], ws[1:]):
            acc = acc + _dot(a_ref[rs, :], w_ref[...])
        accs.append(acc)
    for rs, acc in zip(rows, accs):
        y = _layernorm(alpha * r_ref[rs, :] + acc, g_ref[...], b_ref[...])
        of_ref[rs, :] = y
        ob_ref[rs, :] = y.astype(BF)


def _mm_ln(lhs_list, w_list, resid, g, b, alpha, tm):
    r, d = resid.shape
    n = len(lhs_list)
    row = lambda i: (i, 0)
    const = lambda i: (0, 0)
    return pl.pallas_call(
        functools.partial(_mm_ln_kernel, n_lhs=n, alpha=alpha),
        grid=(r // tm,),
        in_specs=([pl.BlockSpec((tm, a.shape[1]), row) for a in lhs_list]
                  + [pl.BlockSpec(w.shape, const) for w in w_list]
                  + [pl.BlockSpec((tm, d), row), pl.BlockSpec((1, d), const), pl.BlockSpec((1, d), const)]),
        out_specs=[pl.BlockSpec((tm, d), row), pl.BlockSpec((tm, d), row)],
        out_shape=[jax.ShapeDtypeStruct((r, d), F32), jax.ShapeDtypeStruct((r, d), BF)],
        compiler_params=_cparams(1, 48),
        name="proj_ln",
    )(*lhs_list, *w_list, resid, g, b)


def _xattn_kernel(x_ref, wq_ref, mk_ref, mv_ref, o_ref, *, heads, scale):
    x = x_ref[...]
    dh = x.shape[1] // heads
    for h in range(heads):
        hs = slice(h * dh, (h + 1) * dh)
        qm = _dot(x, wq_ref[:, hs]).astype(BF)
        s = _dot_t(qm, mk_ref[:, hs].astype(BF)) * scale
        p = jnp.exp(s - jnp.max(s, axis=1, keepdims=True))
        l = jnp.sum(p, axis=1, keepdims=True)
        o_ref[:, hs] = (_dot(p.astype(BF), mv_ref[:, hs].astype(BF)) / l).astype(BF)


def _xattn(xb, wq, mk, mv, heads, tm):
    r, d = xb.shape
    nmb, n_mem, _ = mk.shape
    mem_idx = (lambda i: (i, 0, 0)) if nmb > 1 else (lambda i: (0, 0, 0))
    return pl.pallas_call(
        functools.partial(_xattn_kernel, heads=heads, scale=(d // heads) ** -0.5),
        grid=(r // tm,),
        in_specs=[pl.BlockSpec((tm, d), lambda i: (i, 0)),
                  pl.BlockSpec((d, d), lambda i: (0, 0)),
                  pl.BlockSpec((None, n_mem, d), mem_idx),
                  pl.BlockSpec((None, n_mem, d), mem_idx)],
        out_specs=pl.BlockSpec((tm, d), lambda i: (i, 0)),
        out_shape=jax.ShapeDtypeStruct((r, d), BF),
        compiler_params=_cparams(1, 48),
        name="mem_xattn",
    )(xb, wq, mk, mv)


def _ffn_kernel(xb_ref, xf_ref, hg_ref, hv_ref, wug_ref, wuv_ref, wcg_ref, wcv_ref, bcg_ref, bcv_ref,
                wd_ref, g_ref, b_ref, o_ref, tg_ref, tv_ref, acc_ref, carry_ref, *hbuf_refs,
                nseg, seg, use_carry, alpha, sub):
    i = pl.program_id(0)
    f = pl.program_id(1)

    @pl.when(f == 0)
    def _():
        acc_ref[...] = jnp.zeros(acc_ref.shape, F32)

    if use_carry:
        @pl.when(i == 0)
        def _():
            carry_ref[0, f] = hg_ref[0]
            carry_ref[1, f] = hv_ref[0]

    xb = xb_ref[...]
    tf = wd_ref.shape[0]

    halves = ((wug_ref, wcg_ref, bcg_ref, hg_ref, tg_ref), (wuv_ref, wcv_ref, bcv_ref, hv_ref, tv_ref))
    subs = [slice(c0, c0 + sub) for c0 in range(0, tf, sub)]

    def slab_of(which, cs):
        return which * len(subs) + cs.start // sub

    for cs in subs:
        for which, (wu_ref, _, _, hist_ref, tail_ref) in enumerate(halves):
            h = _dot(xb, wu_ref[:, cs])
            slab = slab_of(which, cs)
            for b in range(nseg):
                base = b * (seg + FFN_PAD)
                hb = h[b * seg:(b + 1) * seg]
                hist = carry_ref[which, f, :, cs] if use_carry else hist_ref[b, :, cs]
                hbuf_refs[slab][base:base + FFN_PAD, :] = hist
                hbuf_refs[slab][base + FFN_PAD:base + FFN_PAD + seg, :] = hb
                tail = hb[seg - FFN_PAD:seg]
                tail_ref[b, :, cs] = tail
                if use_carry:
                    carry_ref[which, f, :, cs] = tail

    def conv(which, cs):
        _, wc_ref, bc_ref, _, _ = halves[which]
        w = wc_ref[:, cs]
        slab = slab_of(which, cs)
        outs = []
        for b in range(nseg):
            base = b * (seg + FFN_PAD) + FFN_PAD
            p0 = hbuf_refs[slab][base:base + seg, :]
            p1 = hbuf_refs[slab][base - 1:base - 1 + seg, :]
            p2 = hbuf_refs[slab][base - 2:base - 2 + seg, :]
            outs.append(w[0:1, :] * p2 + w[1:2, :] * p1 + w[2:3, :] * p0 + bc_ref[:, cs])
        return outs[0] if nseg == 1 else jnp.concatenate(outs, axis=0)

    down = None
    for cs in subs:
        cg = conv(0, cs)
        cv = conv(1, cs)
        act = (cg * jax.nn.sigmoid(cg) * cv).astype(BF)
        d = _dot(act, wd_ref[cs, :])
        down = d if down is None else down + d
    acc_ref[...] += down

    @pl.when(f == pl.num_programs(1) - 1)
    def _():
        o_ref[...] = _layernorm(alpha * xf_ref[...] + acc_ref[...], g_ref[...], b_ref[...])


def _ffn(xb, xf, hist, wu, wc, bc, wd, g, b, alpha, nseg, seg, use_carry, tf):
    r, d = xf.shape
    tm = nseg * seg
    dff = wd.shape[0]
    nf = dff // tf
    row = lambda i, f: (i, 0)
    const = lambda i, f: (0, 0)
    lo = lambda i, f: (0, f)
    hi = lambda i, f: (0, nf + f)
    lo3 = lambda i, f: (0, 0, f)
    hi3 = lambda i, f: (0, 0, nf + f)
    ntile = r // tm
    sub = 256
    tail = jax.ShapeDtypeStruct((ntile * nseg, FFN_PAD, dff), F32)
    tail_spec = pl.BlockSpec((nseg, FFN_PAD, tf), lambda i, f: (i, 0, f))
    y, tg, tv = pl.pallas_call(
        functools.partial(_ffn_kernel, nseg=nseg, seg=seg, use_carry=use_carry, alpha=alpha, sub=sub),
        grid=(r // tm, nf),
        in_specs=[pl.BlockSpec((tm, d), row), pl.BlockSpec((tm, d), row),
                  pl.BlockSpec((nseg, FFN_PAD, tf), lo3), pl.BlockSpec((nseg, FFN_PAD, tf), hi3),
                  pl.BlockSpec((d, tf), lo), pl.BlockSpec((d, tf), hi),
                  pl.BlockSpec((wc.shape[0], tf), lo), pl.BlockSpec((wc.shape[0], tf), hi),
                  pl.BlockSpec((1, tf), lo), pl.BlockSpec((1, tf), hi),
                  pl.BlockSpec((tf, d), lambda i, f: (f, 0)),
                  pl.BlockSpec((1, d), const), pl.BlockSpec((1, d), const)],
        out_specs=[pl.BlockSpec((tm, d), row), tail_spec, tail_spec],
        out_shape=[jax.ShapeDtypeStruct((r, d), F32), tail, tail],
        scratch_shapes=[pltpu.VMEM((tm, d), F32),
                        pltpu.VMEM((2, nf, FFN_PAD, tf), F32)]
                       + [pltpu.VMEM((nseg * (seg + FFN_PAD), sub), F32)] * (2 * (tf // sub)),
        compiler_params=_cparams(2, 56),
        name="conv_ffn",
    )(xb, xf, hist, hist, wu, wu, wc, wc, bc, bc, wd, g, b)
    last = (ntile - 1) * nseg
    return y, tg[last:], tv[last:]


def _pad_lanes(a, n):
    return jnp.pad(a, ((0, 0), (0, n - a.shape[1])))


def _layer_weights(w_in, b_forget, w_conv, b_conv, g_conv_ln, b_conv_ln, w_out, g_ln1, b_ln1,
                   w_mq, w_mo, g_ln2, b_ln2, w_up, w_ffn_conv, b_ffn_conv, w_down, g_ln3, b_ln3,
                   conv_ch, fox_w):
    assert conv_ch == fox_w, "in-projection reads five equally wide column groups"
    c1, c5 = conv_ch, 2 * conv_ch + 3 * fox_w
    wb = w_in.astype(BF)
    vec = lambda a: a[None, :]
    return dict(
        w_in=wb, wf=_pad_lanes(wb[:, c5:], LANES), bf=_pad_lanes(vec(b_forget), LANES),
        w_conv=jnp.pad(w_conv, ((0, HIST_ROWS - w_conv.shape[0]), (0, 0))
                       ).reshape(HIST_ROWS, -1, LANES).swapaxes(0, 1),
        b_conv=b_conv.reshape(-1, 1, LANES),
        g_conv=vec(g_conv_ln), bl_conv=vec(b_conv_ln),
        wo_a=w_out[:c1].astype(BF), wo_b=w_out[c1:].astype(BF), g1=vec(g_ln1), b1=vec(b_ln1),
        w_mq=w_mq.astype(BF), w_mo=w_mo.astype(BF), g2=vec(g_ln2), b2=vec(b_ln2),
        w_up=w_up.astype(BF), w_fc=w_ffn_conv, b_fc=vec(b_ffn_conv), w_down=w_down.astype(BF),
        g3=vec(g_ln3), b3=vec(b_ln3))


def _run_layer(x, p, cfg, *, nb, conv_hist, fox_cache, mem_k, mem_v, ffn_hist):
    r, d = x.shape
    t = r // nb
    heads, dh, width = cfg["fox_heads"], cfg["fox_head_dim"], cfg["conv_width"]
    tm_in = min(r, 512)
    base2 = LOG2E if fox_cache is None else 1.0
    glu, qb, kf, kb, vf, vb, lf = _inproj(x, p["w_in"], heads * dh, p["wf"], p["bf"],
                                           dh ** -0.5 * base2, tm_in, 512)
    logf = lf[:, :heads]

    if conv_hist is None:
        tt = min(t, 256)
        per = tt // HIST_ROWS
        hist_spec = pl.BlockSpec((HIST_ROWS, glu.shape[1]), lambda i: (jnp.maximum(i * per - 1, 0), 0))
        ca = _conv_module(glu, hist_spec, glu, p["w_conv"], p["b_conv"], p["g_conv"], p["bl_conv"], True, tt, width)
        new_conv = glu[r - (width - 1):][None]
    else:
        hist = jnp.pad(conv_hist, ((0, 0), (HIST_ROWS - (width - 1), 0), (0, 0)))
        hist_spec = pl.BlockSpec((None, HIST_ROWS, glu.shape[1]), lambda i: (i, 0, 0))
        ca = _conv_module(hist, hist_spec, glu, p["w_conv"], p["b_conv"], p["g_conv"], p["bl_conv"], False, t, width)
        conv_in = jnp.concatenate([conv_hist, glu.reshape(nb, t, -1)], axis=1)
        new_conv = conv_in[:, -(width - 1):]

    if fox_cache is None:
        ft = _cumsum_lanes(logf.T, base2).reshape(heads // 2, 2, t)
        ob = _flash_attention(qb, kb, vb, ft, min(t, 512), dh)
    else:
        k_hist, v_hist, lf_hist = fox_cache
        past = k_hist.shape[1]
        lf_all = jnp.concatenate([lf_hist, logf.reshape(nb, t, heads)], axis=1)
        lp = -(-(past + t) // LANES) * LANES
        lf_all = jnp.pad(lf_all, ((0, 0), (0, lp - past - t), (0, 0)))
        ft = _cumsum_lanes(jnp.swapaxes(lf_all, 1, 2).reshape(nb * heads, lp)).reshape(nb, heads // 2, 2, lp)
        ob = _decode_attention(qb, kb, vb, k_hist.astype(BF).reshape(nb, past, -1),
                               v_hist.astype(BF).reshape(nb, past, -1), ft, t, dh)

    tm = min(r, 512)
    x1, x1b = _mm_ln([ca, ob], [p["wo_a"], p["wo_b"]], x, p["g1"], p["b1"], cfg["alpha"], tm)
    om = _xattn(x1b, p["w_mq"], mem_k, mem_v, cfg["mem_heads"], t if mem_k.shape[0] > 1 else min(r, 256))
    x2, x2b = _mm_ln([om], [p["w_mo"]], x1, p["g2"], p["b2"], cfg["alpha"], tm)

    dff2 = p["w_up"].shape[1]
    if ffn_hist is None:
        seg = min(r, 512)
        hist = jnp.zeros((1, FFN_PAD, dff2), F32)
        y, tg, tv = _ffn(x2b, x2, hist, p["w_up"], p["w_fc"], p["b_fc"], p["w_down"], p["g3"], p["b3"],
                         cfg["alpha"], 1, seg, True, 512)
    else:
        hist = jnp.pad(ffn_hist, ((0, 0), (FFN_PAD - ffn_hist.shape[1], 0), (0, 0)))
        y, tg, tv = _ffn(x2b, x2, hist, p["w_up"], p["w_fc"], p["b_fc"], p["w_down"], p["g3"], p["b3"],
                         cfg["alpha"], nb, t, False, 512)
    nhist = cfg["ffn_width"] - 1
    new_ffn = jnp.concatenate([tg, tv], axis=-1)[:, FFN_PAD - nhist:]
    k_new = kf.reshape(nb, t, heads, dh)
    v_new = vf.reshape(nb, t, heads, dh)
    return y, new_conv, k_new, v_new, logf.reshape(nb, t, heads), new_ffn


def kernel(x_prompt, x_sample, cache_conv, cache_fox_k, cache_fox_v, cache_fox_logf, cache_mem_k, cache_mem_v, cache_ffn, mem_prompt, w_in, b_forget, w_conv, b_conv, g_conv_ln, b_conv_ln, w_out, g_ln1, b_ln1, w_mq, w_mk, w_mv, w_mo, g_ln2, b_ln2, w_up, w_ffn_conv, b_ffn_conv, w_down, g_ln3, b_ln3):
    depth = w_in.shape[0]
    bp, tp, d = x_prompt.shape
    bs, ts, _ = x_sample.shape
    assert bp == 1, "prompt path handles a single sequence"
    heads, dh = cache_fox_k.shape[3], cache_fox_k.shape[4]
    n_mem, mem_heads = cache_mem_k.shape[2], cache_mem_k.shape[3]
    conv_ch = w_conv.shape[2]
    cfg = dict(fox_heads=heads, fox_head_dim=dh, conv_width=w_conv.shape[1], mem_heads=mem_heads,
               ffn_width=w_ffn_conv.shape[1], alpha=float((2 * depth) ** 0.25))

    yp = x_prompt.reshape(bp * tp, d)
    ys = x_sample.reshape(bs * ts, d)
    outs = [[] for _ in range(12)]
    for l in range(depth):
        p = _layer_weights(w_in[l], b_forget[l], w_conv[l], b_conv[l], g_conv_ln[l], b_conv_ln[l], w_out[l],
                           g_ln1[l], b_ln1[l], w_mq[l], w_mo[l], g_ln2[l], b_ln2[l], w_up[l], w_ffn_conv[l],
                           b_ffn_conv[l], w_down[l], g_ln3[l], b_ln3[l], conv_ch, heads * dh)
        mem2d = mem_prompt.reshape(bp * n_mem, d)
        mk = _matmul(mem2d, w_mk[l].astype(BF), n_mem, 512)
        mv = _matmul(mem2d, w_mv[l].astype(BF), n_mem, 512)
        yp, c_, k_, v_, lf_, f_ = _run_layer(yp, p, cfg, nb=bp, conv_hist=None, fox_cache=None,
                                             mem_k=mk[None], mem_v=mv[None], ffn_hist=None)
        for o, a in zip(outs[:7], (c_, k_, v_, lf_, mk.reshape(bp, n_mem, mem_heads, -1),
                                   mv.reshape(bp, n_mem, mem_heads, -1), f_)):
            o.append(a)
        ys, c_, k_, v_, lf_, f_ = _run_layer(
            ys, p, cfg, nb=bs, conv_hist=cache_conv[l],
            fox_cache=(cache_fox_k[l], cache_fox_v[l], cache_fox_logf[l]),
            mem_k=cache_mem_k[l].reshape(bs, n_mem, d), mem_v=cache_mem_v[l].reshape(bs, n_mem, d),
            ffn_hist=cache_ffn[l])
        for o, a in zip(outs[7:], (c_, k_, v_, lf_, f_)):
            o.append(a)
    stacked = [jnp.stack(o) for o in outs]
    return (yp.reshape(bp, tp, d), ys.reshape(bs, ts, d), *stacked)
```
